```python
import jax, jax.numpy as jnp
from jax import lax
import numpy as np

D_MODEL = 2048
BATCH = 2
SEQ = 16384
DEPTH = 2

CHUNK = 128
D_SGU = 1024
SGU_HEADS = 8
SGU_HEAD_DIM = D_SGU // SGU_HEADS
D_POOL = 1024
POOL_WINDOWS = (2, 4, 8, 16)
POOL_GROUPS = len(POOL_WINDOWS)
POOL_GROUP_DIM = D_POOL // POOL_GROUPS
D_IN = 2 * D_SGU + D_POOL + 2 * D_MODEL
D_FF = 5632
N_EXPERTS = 8
TOP_K = 2
D_FF_EXPERT = 7168
MOE_BLOCK = 128
N_DENSE = (DEPTH + 1) // 2
N_MOE = DEPTH // 2
EPS = 1e-6

kernel_name = "hybrid_sgu_pool_moe_adaln_trunk"


def _rmsnorm(x, g):
    xf = x.astype(jnp.float32)
    y = xf * lax.rsqrt(jnp.mean(xf * xf, axis=-1, keepdims=True) + EPS)
    return (y * g.astype(jnp.float32)).astype(x.dtype)


def _layernorm(x, g, b):
    xf = x.astype(jnp.float32)
    mu = jnp.mean(xf, axis=-1, keepdims=True)
    var = jnp.mean(jnp.square(xf - mu), axis=-1, keepdims=True)
    y = (xf - mu) * lax.rsqrt(var + EPS) * g.astype(jnp.float32) + b.astype(jnp.float32)
    return y.astype(x.dtype)


def _adaln(c, w, b):
    mod = jax.nn.silu(c) @ w + b
    shift, scale, gate = jnp.split(mod, 3, axis=-1)
    return shift[:, None], scale[:, None], gate[:, None]


def _modulated_norm(x, g, shift, scale):
    return _rmsnorm(x, g) * (1 + scale) + shift


def _chunk_spatial_gate(u, v, w_s, b_s):
    B, S, _ = v.shape
    nc = S // CHUNK
    causal = jnp.tril(jnp.ones((CHUNK, CHUNK), dtype=bool))
    w = jnp.where(causal[None], w_s, 0)
    vc = v.reshape(B, nc, CHUNK, SGU_HEADS, SGU_HEAD_DIM)
    mixed = jnp.einsum('hts,bnshd->bnthd', w, vc) + b_s.T[None, None, :, :, None]
    return u * mixed.reshape(B, S, D_SGU)


def _pool_mixer(p, w_pool, pool_scale):
    B, S, _ = p.shape
    pf = p.astype(jnp.float32)
    cs = jnp.cumsum(pf, axis=1)
    pos = jnp.arange(1, S + 1, dtype=jnp.float32)
    groups = []
    for gi, win in enumerate(POOL_WINDOWS):
        lo, hi = gi * POOL_GROUP_DIM, (gi + 1) * POOL_GROUP_DIM
        csg = cs[..., lo:hi]
        lag = jnp.pad(csg, ((0, 0), (win, 0), (0, 0)))[:, :S]
        count = jnp.minimum(pos, float(win))
        groups.append((csg - lag) / count[None, :, None] - pf[..., lo:hi])
    pooled = jnp.stack(groups, axis=2).astype(p.dtype)
    y = jnp.einsum('bsgc,gcd->bsgd', pooled, w_pool)
    return y.reshape(B, S, D_POOL) * pool_scale


def _token_mixer(h, w_in, ln_g, ln_b, w_s, b_s, w_pool, pool_scale, w_oa, w_ob, w_out):
    z = h @ w_in
    u, v, p, ga, gb = jnp.split(
        z, [D_SGU, 2 * D_SGU, 2 * D_SGU + D_POOL, 2 * D_SGU + D_POOL + D_MODEL], axis=-1)
    u = jax.nn.gelu(u)
    v = _layernorm(jax.nn.gelu(v), ln_g, ln_b)
    y_a = _chunk_spatial_gate(u, v, w_s, b_s) @ w_oa
    y_b = _pool_mixer(p, w_pool, pool_scale) @ w_ob
    merged = jax.nn.sigmoid(ga) * y_a + jax.nn.sigmoid(gb) * y_b
    return merged @ w_out


def _swiglu(h, w1, w3, w2):
    return (jax.nn.silu(h @ w1) * (h @ w3)) @ w2


def _moe(h, w_router, w_gate, w_up, w_down):
    B, S, D = h.shape
    T = B * S
    A = T * TOP_K
    hf = h.reshape(T, D)
    logits = (hf @ w_router).astype(jnp.float32)
    top_logit, top_e = lax.top_k(logits, TOP_K)
    gates = jax.nn.softmax(top_logit, axis=-1)
    e_flat = top_e.reshape(A).astype(jnp.int32)
    t_flat = jnp.repeat(jnp.arange(T, dtype=jnp.int32), TOP_K)
    g_flat = gates.reshape(A)
    order = jnp.argsort(e_flat)
    e_s, t_s, g_s = e_flat[order], t_flat[order], g_flat[order]
    counts = jnp.bincount(e_flat, length=N_EXPERTS).astype(jnp.int32)
    padded = (counts + MOE_BLOCK - 1) // MOE_BLOCK * MOE_BLOCK
    pend = jnp.cumsum(padded)
    pstart = pend - padded
    sstart = jnp.cumsum(counts) - counts
    dest = pstart[e_s] + (jnp.arange(A, dtype=jnp.int32) - sstart[e_s])
    n_blocks = -(-(A + N_EXPERTS * (MOE_BLOCK - 1)) // MOE_BLOCK)
    R = n_blocks * MOE_BLOCK
    row_tok = jnp.zeros((R,), jnp.int32).at[dest].set(t_s)
    row_gate = jnp.zeros((R,), jnp.float32).at[dest].set(g_s)
    blk_start = jnp.arange(n_blocks, dtype=jnp.int32) * MOE_BLOCK
    blk_e = jnp.minimum(jnp.sum(pend[None, :] <= blk_start[:, None], axis=1), N_EXPERTS - 1)

    def expert_block(args):
        tok, gate, e = args
        xb = hf[tok]
        hb = jax.nn.silu(xb @ w_gate[e]) * (xb @ w_up[e])
        out = hb @ w_down[e]
        return out * gate[:, None].astype(out.dtype)

    out = lax.map(expert_block, (row_tok.reshape(n_blocks, MOE_BLOCK),
                                 row_gate.reshape(n_blocks, MOE_BLOCK), blk_e))
    y = jnp.zeros_like(hf).at[row_tok].add(out.reshape(R, D))
    return y.reshape(B, S, D)


def setup_inputs(seed: int = 0) -> dict:
    key = jax.random.key(seed)
    ks = jax.random.split(key, 24)
    f32 = jnp.float32
    nrm = lambda k, shape, s: jax.random.normal(k, shape, f32) * s
    D = D_MODEL
    return {
        "x": nrm(ks[0], (BATCH, SEQ, D), 1.0),
        "c": nrm(ks[1], (BATCH, D), 1.0),
        "w_ada": nrm(ks[2], (DEPTH, 2, D, 3 * D), 0.5 * D ** -0.5),
        "b_ada": nrm(ks[3], (DEPTH, 2, 3 * D), 0.02),
        "norm_g": 1.0 + nrm(ks[4], (DEPTH, 2, D), 0.02),
        "w_in": nrm(ks[5], (DEPTH, D, D_IN), D ** -0.5),
        "ln_g": 1.0 + nrm(ks[6], (DEPTH, D_SGU), 0.02),
        "ln_b": nrm(ks[7], (DEPTH, D_SGU), 0.02),
        "w_s": nrm(ks[8], (DEPTH, SGU_HEADS, CHUNK, CHUNK), CHUNK ** -0.5),
        "b_s": 1.0 + nrm(ks[9], (DEPTH, SGU_HEADS, CHUNK), 0.02),
        "w_pool": nrm(ks[10], (DEPTH, POOL_GROUPS, POOL_GROUP_DIM, POOL_GROUP_DIM), POOL_GROUP_DIM ** -0.5),
        "pool_scale": 1.0 + nrm(ks[11], (DEPTH, D_POOL), 0.02),
        "w_oa": nrm(ks[12], (DEPTH, D_SGU, D), D_SGU ** -0.5),
        "w_ob": nrm(ks[13], (DEPTH, D_POOL, D), D_POOL ** -0.5),
        "w_out": nrm(ks[14], (DEPTH, D, D), D ** -0.5),
        "ffn_w1": nrm(ks[15], (N_DENSE, D, D_FF), D ** -0.5),
        "ffn_w3": nrm(ks[16], (N_DENSE, D, D_FF), D ** -0.5),
        "ffn_w2": nrm(ks[17], (N_DENSE, D_FF, D), D_FF ** -0.5),
        "w_router": nrm(ks[18], (N_MOE, D, N_EXPERTS), D ** -0.5),
        "moe_w_gate": nrm(ks[19], (N_MOE, N_EXPERTS, D, D_FF_EXPERT), D ** -0.5),
        "moe_w_up": nrm(ks[20], (N_MOE, N_EXPERTS, D, D_FF_EXPERT), D ** -0.5),
        "moe_w_down": nrm(ks[21], (N_MOE, N_EXPERTS, D_FF_EXPERT, D), D_FF_EXPERT ** -0.5),
        "final_g": 1.0 + nrm(ks[22], (D,), 0.02),
    }


def reference(x, c, w_ada, b_ada, norm_g, w_in, ln_g, ln_b, w_s, b_s, w_pool, pool_scale,
              w_oa, w_ob, w_out, ffn_w1, ffn_w3, ffn_w2, w_router, moe_w_gate, moe_w_up,
              moe_w_down, final_g):
    for i in range(DEPTH):
        shift, scale, gate = _adaln(c, w_ada[i, 0], b_ada[i, 0])
        h = _modulated_norm(x, norm_g[i, 0], shift, scale)
        x = x + gate * _token_mixer(h, w_in[i], ln_g[i], ln_b[i], w_s[i], b_s[i],
                                    w_pool[i], pool_scale[i], w_oa[i], w_ob[i], w_out[i])
        shift, scale, gate = _adaln(c, w_ada[i, 1], b_ada[i, 1])
        h = _modulated_norm(x, norm_g[i, 1], shift, scale)
        j = i // 2
        if i % 2 == 0:
            f = _swiglu(h, ffn_w1[j], ffn_w3[j], ffn_w2[j])
        else:
            f = _moe(h, w_router[j], moe_w_gate[j], moe_w_up[j], moe_w_down[j])
        x = x + gate * f
    return _rmsnorm(x, final_g)
```

```python
import functools

import jax
import jax.numpy as jnp
from jax import lax
from jax.experimental import pallas as pl
from jax.experimental.pallas import tpu as pltpu

EPS = 1e-6
POOL_WINDOWS = (2, 4, 8, 16)
TOP_K = 2
POOL_HALO = 16
V7X_VMEM_LIMIT_BYTES = 56 * 1024 * 1024

_BF16 = jnp.bfloat16
_F32 = jnp.float32


def _cparams(sem):
    return pltpu.CompilerParams(dimension_semantics=sem, vmem_limit_bytes=V7X_VMEM_LIMIT_BYTES)


def _gelu_tanh(x):
    return 0.5 * x * (1.0 + jnp.tanh(0.7978845608028654 * (x + 0.044715 * (x * x * x))))


def _silu(x):
    return x * jax.nn.sigmoid(x)


def _modnorm(x, g, shift, scale):
    y = x * lax.rsqrt(jnp.mean(x * x, axis=-1, keepdims=True) + EPS)
    return (y * g) * (1.0 + scale) + shift


def _pick(n, pref):
    for t in pref:
        if n % t == 0:
            return t
    return n


def _ada_kernel(c_ref, w_ref, b_ref, o_ref):
    sc = _silu(c_ref[...]).astype(_BF16)
    o_ref[0] = jnp.dot(sc, w_ref[0].astype(_BF16), preferred_element_type=_F32) + b_ref[0]


def _adaln_all(c, w_ada, b_ada):
    ns, d, d3 = w_ada.shape
    b = c.shape[0]
    tn = _pick(d3, (768, 512, 256, 128))
    return pl.pallas_call(
        _ada_kernel,
        grid=(ns, d3 // tn),
        in_specs=[
            pl.BlockSpec((b, d), lambda s, j: (0, 0)),
            pl.BlockSpec((1, d, tn), lambda s, j: (s, 0, j)),
            pl.BlockSpec((1, 1, tn), lambda s, j: (s, 0, j)),
        ],
        out_specs=pl.BlockSpec((1, b, tn), lambda s, j: (s, 0, j)),
        out_shape=jax.ShapeDtypeStruct((ns, b, d3), _F32),
        compiler_params=_cparams(("arbitrary", "arbitrary")),
        name="adaln_mod",
    )(c, w_ada, b_ada)


def _inproj_kernel(x_ref, mod_ref, g_ref, w_ref, lng_ref, lnb_ref, ws_ref, bst_ref, wp_ref, ps_ref,
                   a_ref, b_ref, sga_ref, sgb_ref,
                   h_ref, u_ref, vn_ref, halo_ref,
                   *, tiles_per_seq, n_gate_tiles, heads, chunk, windows):
    i = pl.program_id(0)
    j = pl.program_id(1)
    tm = x_ref.shape[0]
    tn = w_ref.shape[1]

    @pl.when(j == 0)
    def _():
        m = mod_ref[0]
        h_ref[...] = _modnorm(x_ref[...], g_ref[...], m[0:1, :], m[1:2, :]).astype(_BF16)

    z = jnp.dot(h_ref[...], w_ref[...], preferred_element_type=_F32)

    @pl.when(j == 0)
    def _():
        u_ref[...] = _gelu_tanh(z).astype(u_ref.dtype)

    @pl.when(j == 1)
    def _():
        v = _gelu_tanh(z)
        mu = jnp.mean(v, axis=-1, keepdims=True)
        vc = v - mu
        var = jnp.mean(vc * vc, axis=-1, keepdims=True)
        vn_ref[...] = (vc * lax.rsqrt(var + EPS) * lng_ref[...] + lnb_ref[...]).astype(_BF16)
        hd = tn // heads
        row = lax.broadcasted_iota(jnp.int32, (chunk, chunk), 0)
        col = lax.broadcasted_iota(jnp.int32, (chunk, chunk), 1)
        causal = col <= row
        for hh in range(heads):
            wm = jnp.where(causal, ws_ref[hh], 0.0).astype(_BF16)
            bias = bst_ref[:, hh:hh + 1]
            for cc in range(tm // chunk):
                rs = slice(cc * chunk, (cc + 1) * chunk)
                cs = slice(hh * hd, (hh + 1) * hd)
                mixed = jnp.dot(wm, vn_ref[rs, cs], preferred_element_type=_F32) + bias
                a_ref[rs, cs] = (u_ref[rs, cs].astype(_F32) * mixed).astype(a_ref.dtype)

    @pl.when(j == 2)
    def _():
        first = (i % tiles_per_seq) == 0
        hist = jnp.where(first, 0.0, halo_ref[...])
        ext = jnp.concatenate([hist, z], axis=0)
        halo_ref[...] = z[tm - POOL_HALO:, :]
        pos = (i % tiles_per_seq) * tm + lax.broadcasted_iota(jnp.int32, (tm, 1), 0) + 1
        gd = tn // len(windows)
        for gi, win in enumerate(windows):
            cs = slice(gi * gd, (gi + 1) * gd)
            acc = ext[:, cs]
            step = 1
            while step < win:
                acc = acc + pltpu.roll(acc, step, 0)
                step *= 2
            cnt = jnp.minimum(pos, win).astype(_F32)
            pooled = acc[POOL_HALO:, :] / cnt - z[:, cs]
            y = jnp.dot(pooled.astype(_BF16), wp_ref[gi], preferred_element_type=_F32)
            b_ref[:, cs] = (y * ps_ref[:, cs]).astype(b_ref.dtype)

    for t in range(n_gate_tiles):
        @pl.when(j == 3 + t)
        def _(t=t):
            sga_ref[:, t * tn:(t + 1) * tn] = jax.nn.sigmoid(z).astype(sga_ref.dtype)

        @pl.when(j == 3 + n_gate_tiles + t)
        def _(t=t):
            sgb_ref[:, t * tn:(t + 1) * tn] = jax.nn.sigmoid(z).astype(sgb_ref.dtype)


def _inproj(x2, mod, norm_g, w_in, ln_g, ln_b, w_s, b_s_t, w_pool, pool_scale, seq):
    t, d = x2.shape
    d_in = w_in.shape[1]
    tn = ln_g.shape[-1]
    heads, chunk, _ = w_s.shape
    groups, gd, _ = w_pool.shape
    assert pool_scale.shape[-1] == tn and groups * gd == tn and d % tn == 0
    assert d_in == 3 * tn + 2 * d and max(POOL_WINDOWS[:groups]) <= POOL_HALO
    tm = _pick(seq, (512, 256, 128))
    assert tm % chunk == 0 and tm >= POOL_HALO
    n_gate_tiles = d // tn
    kern = functools.partial(_inproj_kernel, tiles_per_seq=seq // tm, n_gate_tiles=n_gate_tiles,
                             heads=heads, chunk=chunk, windows=POOL_WINDOWS[:groups])
    tps = seq // tm
    const2 = lambda i, j: (0, 0)
    const3 = lambda i, j: (0, 0, 0)
    return pl.pallas_call(
        kern,
        grid=(t // tm, d_in // tn),
        in_specs=[
            pl.BlockSpec((tm, d), lambda i, j: (i, 0)),
            pl.BlockSpec((1, 3, d), lambda i, j: (i // tps, 0, 0)),
            pl.BlockSpec((1, d), const2),
            pl.BlockSpec((d, tn), lambda i, j: (0, j)),
            pl.BlockSpec((1, tn), const2),
            pl.BlockSpec((1, tn), const2),
            pl.BlockSpec((heads, chunk, chunk), const3),
            pl.BlockSpec((chunk, heads), const2),
            pl.BlockSpec((groups, gd, gd), const3),
            pl.BlockSpec((1, tn), const2),
        ],
        out_specs=[
            pl.BlockSpec((tm, tn), lambda i, j: (i, 0)),
            pl.BlockSpec((tm, tn), lambda i, j: (i, 0)),
            pl.BlockSpec((tm, d), lambda i, j: (i, 0)),
            pl.BlockSpec((tm, d), lambda i, j: (i, 0)),
        ],
        out_shape=[
            jax.ShapeDtypeStruct((t, tn), _BF16),
            jax.ShapeDtypeStruct((t, tn), _BF16),
            jax.ShapeDtypeStruct((t, d), _BF16),
            jax.ShapeDtypeStruct((t, d), _BF16),
        ],
        scratch_shapes=[
            pltpu.VMEM((tm, d), _BF16),
            pltpu.VMEM((tm, tn), _BF16),
            pltpu.VMEM((tm, tn), _BF16),
            pltpu.VMEM((POOL_HALO, tn), _F32),
        ],
        compiler_params=_cparams(("arbitrary", "arbitrary")),
        name="mixer_inproj",
    )(x2, mod, norm_g, w_in, ln_g, ln_b, w_s, b_s_t, w_pool, pool_scale)


def _outproj_kernel(a_ref, b_ref, sga_ref, sgb_ref, x_ref, mod_ref, woa_ref, wob_ref, wout_ref,
                    o_ref, m_ref, *, tc):
    d = x_ref.shape[1]
    for c in range(d // tc):
        cs = slice(c * tc, (c + 1) * tc)
        ya = jnp.dot(a_ref[...], woa_ref[:, cs], preferred_element_type=_F32)
        yb = jnp.dot(b_ref[...], wob_ref[:, cs], preferred_element_type=_F32)
        m_ref[:, cs] = (sga_ref[:, cs].astype(_F32) * ya + sgb_ref[:, cs].astype(_F32) * yb).astype(_BF16)
    gate = mod_ref[0][2:3, :]
    for c in range(d // tc):
        cs = slice(c * tc, (c + 1) * tc)
        y = jnp.dot(m_ref[...], wout_ref[:, cs], preferred_element_type=_F32)
        o_ref[:, cs] = x_ref[:, cs] + gate[:, cs] * y


def _outproj(a, b, sga, sgb, x2, mod, w_oa, w_ob, w_out, seq):
    t, d = x2.shape
    ds = a.shape[1]
    tm = _pick(seq, (512, 256, 128))
    tc = _pick(d, (512, 256, 128))
    tps = seq // tm
    once = pl.Buffered(1)
    return pl.pallas_call(
        functools.partial(_outproj_kernel, tc=tc),
        grid=(t // tm,),
        in_specs=[
            pl.BlockSpec((tm, ds), lambda i: (i, 0)),
            pl.BlockSpec((tm, ds), lambda i: (i, 0)),
            pl.BlockSpec((tm, d), lambda i: (i, 0)),
            pl.BlockSpec((tm, d), lambda i: (i, 0)),
            pl.BlockSpec((tm, d), lambda i: (i, 0)),
            pl.BlockSpec((1, 3, d), lambda i: (i // tps, 0, 0)),
            pl.BlockSpec((ds, d), lambda i: (0, 0), pipeline_mode=once),
            pl.BlockSpec((ds, d), lambda i: (0, 0), pipeline_mode=once),
            pl.BlockSpec((d, d), lambda i: (0, 0), pipeline_mode=once),
        ],
        out_specs=pl.BlockSpec((tm, d), lambda i: (i, 0)),
        out_shape=jax.ShapeDtypeStruct((t, d), _F32),
        scratch_shapes=[pltpu.VMEM((tm, d), _BF16)],
        compiler_params=_cparams(("arbitrary",)),
        name="mixer_outproj",
    )(a, b, sga, sgb, x2, mod, w_oa, w_ob, w_out)


def _ffn_kernel(x_ref, mod_ref, g_ref, w1_ref, w3_ref, w2_ref, o_ref, h_ref):
    k = pl.program_id(1)
    m = mod_ref[0]

    @pl.when(k == 0)
    def _():
        h_ref[...] = _modnorm(x_ref[...], g_ref[...], m[0:1, :], m[1:2, :]).astype(_BF16)

    h = h_ref[...]
    gte = jnp.dot(h, w1_ref[...], preferred_element_type=_F32)
    up = jnp.dot(h, w3_ref[...], preferred_element_type=_F32)
    act = (_silu(gte) * up).astype(_BF16)
    y = jnp.dot(act, w2_ref[...], preferred_element_type=_F32)

    @pl.when(k == 0)
    def _():
        o_ref[...] = y

    @pl.when(k > 0)
    def _():
        o_ref[...] += y

    @pl.when(k == pl.num_programs(1) - 1)
    def _():
        o_ref[...] = x_ref[...] + m[2:3, :] * o_ref[...]


def _dense_ffn(x2, mod, norm_g, w1, w3, w2, seq):
    t, d = x2.shape
    f = w1.shape[1]
    tm = _pick(seq, (512, 256, 128))
    tf = _pick(f, (512, 256, 128))
    tps = seq // tm
    return pl.pallas_call(
        _ffn_kernel,
        grid=(t // tm, f // tf),
        in_specs=[
            pl.BlockSpec((tm, d), lambda i, k: (i, 0)),
            pl.BlockSpec((1, 3, d), lambda i, k: (i // tps, 0, 0)),
            pl.BlockSpec((1, d), lambda i, k: (0, 0)),
            pl.BlockSpec((d, tf), lambda i, k: (0, k)),
            pl.BlockSpec((d, tf), lambda i, k: (0, k)),
            pl.BlockSpec((tf, d), lambda i, k: (k, 0)),
        ],
        out_specs=pl.BlockSpec((tm, d), lambda i, k: (i, 0)),
        out_shape=jax.ShapeDtypeStruct((t, d), _F32),
        scratch_shapes=[pltpu.VMEM((tm, d), _BF16)],
        compiler_params=_cparams(("arbitrary", "arbitrary")),
        name="dense_swiglu",
    )(x2, mod, norm_g, w1, w3, w2)


def _route_kernel(x_ref, mod_ref, g_ref, wrt_ref, h_ref, route_ref, cnt_ref, run_ref):
    i = pl.program_id(0)
    tm = x_ref.shape[0]
    ne = wrt_ref.shape[0]

    @pl.when(i == 0)
    def _():
        run_ref[...] = jnp.zeros_like(run_ref)

    m = mod_ref[0]
    h = _modnorm(x_ref[...], g_ref[...], m[0:1, :], m[1:2, :])
    h_ref[...] = h
    lt = lax.dot_general(wrt_ref[...], h.astype(_BF16), (((1,), (1,)), ((), ())),
                         preferred_element_type=_F32)
    eid = lax.broadcasted_iota(jnp.int32, (ne, tm), 0).astype(_F32)
    m1 = jnp.max(lt, axis=0, keepdims=True)
    i1 = jnp.min(jnp.where(lt == m1, eid, float(ne)), axis=0, keepdims=True)
    l2 = jnp.where(eid == i1, -jnp.inf, lt)
    m2 = jnp.max(l2, axis=0, keepdims=True)
    i2 = jnp.min(jnp.where(l2 == m2, eid, float(ne)), axis=0, keepdims=True)
    e2 = jnp.exp(m2 - m1)
    g1 = 1.0 / (1.0 + e2)
    g2 = e2 / (1.0 + e2)
    oh1 = eid == i1
    oh2 = eid == i2
    both = jnp.where(oh1 | oh2, 1.0, 0.0).astype(_BF16)
    r = lax.broadcasted_iota(jnp.int32, (tm, tm), 0)
    c = lax.broadcasted_iota(jnp.int32, (tm, tm), 1)
    upper = jnp.where(r <= c, 1.0, 0.0).astype(_BF16)
    pref = jnp.dot(both, upper, preferred_element_type=_F32)
    base = run_ref[:, 0:1] + pref - 1.0
    rank1 = jnp.sum(jnp.where(oh1, base, 0.0), axis=0, keepdims=True)
    rank2 = jnp.sum(jnp.where(oh2, base, 0.0), axis=0, keepdims=True)
    run_new = run_ref[:, 0:1] + pref[:, tm - 1:tm]
    run_ref[...] = jnp.broadcast_to(run_new, run_ref.shape)
    cnt_ref[...] = jnp.broadcast_to(run_new, cnt_ref.shape)
    route_ref[0:1, :] = i1
    route_ref[1:2, :] = i2
    route_ref[2:3, :] = rank1
    route_ref[3:4, :] = rank2
    route_ref[4:5, :] = g1
    route_ref[5:6, :] = g2
    route_ref[6:8, :] = jnp.zeros((2, tm), _F32)


def _route(x2, mod, norm_g, w_router_t, seq):
    t, d = x2.shape
    ne = w_router_t.shape[0]
    tm = _pick(seq, (512, 256, 128))
    tps = seq // tm
    return pl.pallas_call(
        _route_kernel,
        grid=(t // tm,),
        in_specs=[
            pl.BlockSpec((tm, d), lambda i: (i, 0)),
            pl.BlockSpec((1, 3, d), lambda i: (i // tps, 0, 0)),
            pl.BlockSpec((1, d), lambda i: (0, 0)),
            pl.BlockSpec((ne, d), lambda i: (0, 0)),
        ],
        out_specs=[
            pl.BlockSpec((tm, d), lambda i: (i, 0)),
            pl.BlockSpec((8, tm), lambda i: (0, i)),
            pl.BlockSpec((ne, 128), lambda i: (0, 0)),
        ],
        out_shape=[
            jax.ShapeDtypeStruct((t, d), _F32),
            jax.ShapeDtypeStruct((8, t), _F32),
            jax.ShapeDtypeStruct((ne, 128), _F32),
        ],
        scratch_shapes=[pltpu.VMEM((ne, 128), _F32)],
        compiler_params=_cparams(("arbitrary",)),
        name="moe_route",
    )(x2, mod, norm_g, w_router_t)


def _row_copy(src_ref, src_row, dst_ref, dst_row, sem):
    return pltpu.make_async_copy(src_ref.at[pl.ds(src_row, 1)], dst_ref.at[pl.ds(dst_row, 1)], sem)


def _dispatch_kernel(dest_ref, h_ref, xs_ref, sem):
    tm = h_ref.shape[0]

    def issue(r, carry):
        for kk in range(TOP_K):
            _row_copy(h_ref, r, xs_ref, dest_ref[0, 0, kk * tm + r], sem).start()
        return carry

    lax.fori_loop(0, tm, issue, 0)

    def drain(r, carry):
        for kk in range(TOP_K):
            _row_copy(h_ref, r, xs_ref, 0, sem).wait()
        return carry

    lax.fori_loop(0, tm, drain, 0)


def _dispatch(h, dest_tiles, n_rows):
    t, d = h.shape
    tm = dest_tiles.shape[-1] // TOP_K
    return pl.pallas_call(
        _dispatch_kernel,
        grid=(t // tm,),
        in_specs=[
            pl.BlockSpec((1, 1, TOP_K * tm), lambda i: (i, 0, 0), memory_space=pltpu.SMEM),
            pl.BlockSpec((tm, d), lambda i: (i, 0)),
        ],
        out_specs=pl.BlockSpec(memory_space=pl.ANY),
        out_shape=jax.ShapeDtypeStruct((n_rows, d), _F32),
        scratch_shapes=[pltpu.SemaphoreType.DMA(())],
        compiler_params=_cparams(("arbitrary",)),
        name="moe_dispatch",
    )(dest_tiles, h)


def _combine_kernel(dest_ref, ys_ref, gates_ref, x_ref, mod_ref, fg_ref, o_ref, buf_ref, sem, *, final_norm):
    tm = x_ref.shape[0]

    def issue(r, carry):
        for kk in range(TOP_K):
            _row_copy(ys_ref, dest_ref[0, 0, kk * tm + r], buf_ref.at[kk], r, sem).start()
        return carry

    lax.fori_loop(0, tm, issue, 0)

    def drain(r, carry):
        for kk in range(TOP_K):
            _row_copy(ys_ref, 0, buf_ref.at[kk], r, sem).wait()
        return carry

    lax.fori_loop(0, tm, drain, 0)
    y = gates_ref[:, 0:1] * buf_ref[0]
    for kk in range(1, TOP_K):
        y = y + gates_ref[:, kk:kk + 1] * buf_ref[kk]
    xn = x_ref[...] + mod_ref[0][2:3, :] * y
    if final_norm:
        xn = xn * lax.rsqrt(jnp.mean(xn * xn, axis=-1, keepdims=True) + EPS) * fg_ref[...]
    o_ref[...] = xn


def _combine(ys, dest_tiles, gates, x2, mod, final_g, seq, final_norm):
    t, d = x2.shape
    tm = dest_tiles.shape[-1] // TOP_K
    tps = seq // tm
    return pl.pallas_call(
        functools.partial(_combine_kernel, final_norm=final_norm),
        grid=(t // tm,),
        in_specs=[
            pl.BlockSpec((1, 1, TOP_K * tm), lambda i: (i, 0, 0), memory_space=pltpu.SMEM),
            pl.BlockSpec(memory_space=pl.ANY),
            pl.BlockSpec((tm, TOP_K), lambda i: (i, 0)),
            pl.BlockSpec((tm, d), lambda i: (i, 0)),
            pl.BlockSpec((1, 3, d), lambda i: (i // tps, 0, 0)),
            pl.BlockSpec((1, d), lambda i: (0, 0)),
        ],
        out_specs=pl.BlockSpec((tm, d), lambda i: (i, 0)),
        out_shape=jax.ShapeDtypeStruct((t, d), _F32),
        scratch_shapes=[pltpu.VMEM((TOP_K, tm, d), _F32), pltpu.SemaphoreType.DMA(())],
        compiler_params=_cparams(("arbitrary",)),
        name="moe_combine",
    )(dest_tiles, ys, gates, x2, mod, final_g)


def _experts_kernel(te_ref, nv_ref, lim_ref, xs_ref, wg_ref, wu_ref, wd_ref, o_ref, h_ref):
    i = pl.program_id(0)
    k = pl.program_id(1)
    tm = xs_ref.shape[0]

    @pl.when(i < nv_ref[0])
    def _():
        @pl.when(k == 0)
        def _():
            rows = i * tm + lax.broadcasted_iota(jnp.int32, (tm, 1), 0)
            h_ref[...] = jnp.where(rows < lim_ref[i], xs_ref[...], 0.0).astype(_BF16)

        h = h_ref[...]
        gte = jnp.dot(h, wg_ref[0], preferred_element_type=_F32)
        up = jnp.dot(h, wu_ref[0], preferred_element_type=_F32)
        act = (_silu(gte) * up).astype(_BF16)
        y = jnp.dot(act, wd_ref[0], preferred_element_type=_F32)

        @pl.when(k == 0)
        def _():
            o_ref[...] = y

        @pl.when(k > 0)
        def _():
            o_ref[...] += y


def _experts(xs, tile_expert, n_valid, row_limit, w_gate, w_up, w_down, tm):
    r, d = xs.shape
    ne, _, f = w_gate.shape
    tf = _pick(f, (512, 256, 128))
    nk = f // tf
    n_tiles = r // tm

    def row_map(i, k, te, nv, lim):
        return (jnp.minimum(i, nv[0] - 1), 0)

    def kk(i, k, nv):
        return jnp.where(i < nv[0], k, nk - 1)

    return pl.pallas_call(
        _experts_kernel,
        grid_spec=pltpu.PrefetchScalarGridSpec(
            num_scalar_prefetch=3,
            grid=(n_tiles, nk),
            in_specs=[
                pl.BlockSpec((tm, d), row_map),
                pl.BlockSpec((1, d, tf), lambda i, k, te, nv, lim: (te[i], 0, kk(i, k, nv))),
                pl.BlockSpec((1, d, tf), lambda i, k, te, nv, lim: (te[i], 0, kk(i, k, nv))),
                pl.BlockSpec((1, tf, d), lambda i, k, te, nv, lim: (te[i], kk(i, k, nv), 0)),
            ],
            out_specs=pl.BlockSpec((tm, d), row_map),
            scratch_shapes=[pltpu.VMEM((tm, d), _BF16)],
        ),
        out_shape=jax.ShapeDtypeStruct((r, d), _F32),
        compiler_params=_cparams(("arbitrary", "arbitrary")),
        name="moe_experts",
    )(tile_expert, n_valid, row_limit, xs, w_gate, w_up, w_down)


def _moe_layer(x2, mod, norm_g, w_router, w_gate, w_up, w_down, final_g, seq, final_norm):
    t, d = x2.shape
    ne = w_router.shape[1]
    h, route, cnt = _route(x2, mod, norm_g, w_router.T.astype(_BF16), seq)
    tm = _pick(seq, (512, 256, 128))
    counts = cnt[:, 0].astype(jnp.int32)
    padded = (counts + tm - 1) // tm * tm
    pend = jnp.cumsum(padded)
    pstart = pend - padded
    idx = route[0:TOP_K].astype(jnp.int32)
    rank = route[TOP_K:2 * TOP_K].astype(jnp.int32)
    dest = pstart[idx] + rank
    gates = route[2 * TOP_K:3 * TOP_K].T
    n_tiles = (TOP_K * t) // tm + ne
    dest_tiles = dest.reshape(TOP_K, t // tm, tm).transpose(1, 0, 2).reshape(t // tm, 1, TOP_K * tm)
    tile_start = jnp.arange(n_tiles, dtype=jnp.int32) * tm
    tile_expert = jnp.minimum(jnp.sum(pend[None, :] <= tile_start[:, None], axis=1), ne - 1).astype(jnp.int32)
    n_valid = (pend[-1] // tm).astype(jnp.int32).reshape(1)
    row_limit = (pstart + counts)[tile_expert].astype(jnp.int32)
    xs = _dispatch(h, dest_tiles, n_tiles * tm)
    ys = _experts(xs, tile_expert, n_valid, row_limit, w_gate, w_up, w_down, tm)
    return _combine(ys, dest_tiles, gates, x2, mod, final_g, seq, final_norm)


def kernel(x, c, w_ada, b_ada, norm_g, w_in, ln_g, ln_b, w_s, b_s, w_pool, pool_scale, w_oa, w_ob, w_out,
           ffn_w1, ffn_w3, ffn_w2, w_router, moe_w_gate, moe_w_up, moe_w_down, final_g):
    bsz, seq, d = x.shape
    depth = w_in.shape[0]
    t = bsz * seq
    x2 = x.reshape(t, d)
    mods = _adaln_all(c, w_ada.reshape(depth * 2, d, 3 * d), b_ada.reshape(depth * 2, 1, 3 * d))
    mods = mods.reshape(depth * 2, bsz, 3, d)
    fg = final_g.reshape(1, d)
    bf = lambda w: w.astype(_BF16)
    for i in range(depth):
        last = i == depth - 1
        a, b, sga, sgb = _inproj(x2, mods[2 * i], norm_g[i, 0].reshape(1, d), bf(w_in[i]),
                                 ln_g[i].reshape(1, -1), ln_b[i].reshape(1, -1), w_s[i], b_s[i].T,
                                 bf(w_pool[i]), pool_scale[i].reshape(1, -1), seq)
        x2 = _outproj(a, b, sga, sgb, x2, mods[2 * i], bf(w_oa[i]), bf(w_ob[i]), bf(w_out[i]), seq)
        j = i // 2
        ng = norm_g[i, 1].reshape(1, d)
        if i % 2 == 0:
            x2 = _dense_ffn(x2, mods[2 * i + 1], ng, bf(ffn_w1[j]), bf(ffn_w3[j]), bf(ffn_w2[j]), seq)
            if last:
                x2 = _final_norm(x2, fg)
        else:
            x2 = _moe_layer(x2, mods[2 * i + 1], ng, w_router[j], bf(moe_w_gate[j]), bf(moe_w_up[j]),
                            bf(moe_w_down[j]), fg, seq, final_norm=last)
    return x2.reshape(bsz, seq, d)


def _final_norm_kernel(x_ref, g_ref, o_ref):
    x = x_ref[...]
    o_ref[...] = x * lax.rsqrt(jnp.mean(x * x, axis=-1, keepdims=True) + EPS) * g_ref[...]


def _final_norm(x2, fg):
    t, d = x2.shape
    tm = _pick(t, (512, 256, 128))
    return pl.pallas_call(
        _final_norm_kernel,
        grid=(t // tm,),
        in_specs=[pl.BlockSpec((tm, d), lambda i: (i, 0)), pl.BlockSpec((1, d), lambda i: (0, 0))],
        out_specs=pl.BlockSpec((tm, d), lambda i: (i, 0)),
        out_shape=jax.ShapeDtypeStruct((t, d), _F32),
        compiler_params=_cparams(("arbitrary",)),
        name="final_norm",
    )(x2, fg)
```

```python
import functools

import jax
import jax.numpy as jnp
from jax import lax
from jax.experimental import pallas as pl
from jax.experimental.pallas import tpu as pltpu

EPS = 1e-6
POOL_WINDOWS = (2, 4, 8, 16)
TOP_K = 2
POOL_HALO = 16
V7X_MXU_COLS = 256
V7X_VMEM_LIMIT_BYTES = 56 * 1024 * 1024
DMA_ISSUE_UNROLL = 8

_BF16 = jnp.bfloat16
_F32 = jnp.float32


def _cparams(sem):
    return pltpu.CompilerParams(dimension_semantics=sem, vmem_limit_bytes=V7X_VMEM_LIMIT_BYTES)


def _gelu_tanh(x):
    return 0.5 * x * (1.0 + jnp.tanh(0.7978845608028654 * (x + 0.044715 * (x * x * x))))


def _silu(x):
    return x * jax.nn.sigmoid(x)


def _modnorm(x, g, shift, scale):
    y = x * lax.rsqrt(jnp.mean(x * x, axis=-1, keepdims=True) + EPS)
    return (y * g) * (1.0 + scale) + shift


def _pick(n, pref):
    for t in pref:
        if n % t == 0:
            return t
    return n


def _ada_kernel(c_ref, w_ref, b_ref, o_ref):
    sc = _silu(c_ref[...]).astype(_BF16)
    o_ref[0] = jnp.dot(sc, w_ref[0].astype(_BF16), preferred_element_type=_F32) + b_ref[0]


def _adaln_all(c, w_ada, b_ada):
    ns, d, d3 = w_ada.shape
    b = c.shape[0]
    tn = _pick(d3, (768, 512, 256, 128))
    return pl.pallas_call(
        _ada_kernel,
        grid=(ns, d3 // tn),
        in_specs=[
            pl.BlockSpec((b, d), lambda s, j: (0, 0)),
            pl.BlockSpec((1, d, tn), lambda s, j: (s, 0, j)),
            pl.BlockSpec((1, 1, tn), lambda s, j: (s, 0, j)),
        ],
        out_specs=pl.BlockSpec((1, b, tn), lambda s, j: (s, 0, j)),
        out_shape=jax.ShapeDtypeStruct((ns, b, d3), _F32),
        compiler_params=_cparams(("arbitrary", "arbitrary")),
        name="adaln_mod",
    )(c, w_ada, b_ada)


def _inproj_kernel(x_ref, mod_ref, g_ref, w_ref, lng_ref, lnb_ref, ws_ref, bst_ref, wp_ref, ps_ref,
                   a_ref, b_ref, gates_ref,
                   h_ref, u_ref, v_ref, vn_ref, halo_ref,
                   *, tiles_per_seq, heads, chunk, windows):
    i = pl.program_id(0)
    j = pl.program_id(1)
    tm = x_ref.shape[0]
    tn = w_ref.shape[1]
    sw = min(V7X_MXU_COLS, tn)

    def zslab(lo, width):
        return jnp.dot(h_ref[...], w_ref[:, lo:lo + width], preferred_element_type=_F32)

    @pl.when(j == 0)
    def _():
        m = mod_ref[0]
        h_ref[...] = _modnorm(x_ref[...], g_ref[...], m[0:1, :], m[1:2, :]).astype(_BF16)
        for lo in range(0, tn, sw):
            u_ref[:, lo:lo + sw] = _gelu_tanh(zslab(lo, sw)).astype(u_ref.dtype)

    @pl.when(j == 1)
    def _():
        rsum = jnp.zeros((tm, 1), _F32)
        for lo in range(0, tn, sw):
            v = _gelu_tanh(zslab(lo, sw))
            v_ref[:, lo:lo + sw] = v
            rsum = rsum + jnp.sum(v, axis=-1, keepdims=True)
        mu = rsum * (1.0 / tn)
        vsum = jnp.zeros((tm, 1), _F32)
        for lo in range(0, tn, sw):
            vc = v_ref[:, lo:lo + sw] - mu
            vsum = vsum + jnp.sum(vc * vc, axis=-1, keepdims=True)
        rstd = lax.rsqrt(vsum * (1.0 / tn) + EPS)
        for lo in range(0, tn, sw):
            cs = slice(lo, lo + sw)
            vn_ref[:, cs] = ((v_ref[:, cs] - mu) * rstd * lng_ref[:, cs] + lnb_ref[:, cs]).astype(_BF16)
        hd = tn // heads
        row = lax.broadcasted_iota(jnp.int32, (chunk, chunk), 0)
        col = lax.broadcasted_iota(jnp.int32, (chunk, chunk), 1)
        causal = col <= row
        for hh in range(heads):
            wm = jnp.where(causal, ws_ref[hh], 0.0).astype(_BF16)
            bias = bst_ref[:, hh:hh + 1]
            for cc in range(tm // chunk):
                rs = slice(cc * chunk, (cc + 1) * chunk)
                cs = slice(hh * hd, (hh + 1) * hd)
                mixed = jnp.dot(wm, vn_ref[rs, cs], preferred_element_type=_F32) + bias
                a_ref[rs, cs] = (u_ref[rs, cs].astype(_F32) * mixed).astype(a_ref.dtype)

    @pl.when(j == 2)
    def _():
        for lo in range(0, tn, sw):
            v_ref[:, lo:lo + sw] = zslab(lo, sw)
        first = (i % tiles_per_seq) == 0
        pos = (i % tiles_per_seq) * tm + lax.broadcasted_iota(jnp.int32, (tm, 1), 0) + 1
        gd = tn // len(windows)
        for gi, win in enumerate(windows):
            cs = slice(gi * gd, (gi + 1) * gd)
            z = v_ref[:, cs]
            hist = jnp.where(first, 0.0, halo_ref[:, cs])
            acc = jnp.concatenate([hist, z], axis=0)
            halo_ref[:, cs] = z[tm - POOL_HALO:, :]
            step = 1
            while step < win:
                acc = acc + pltpu.roll(acc, step, 0)
                step *= 2
            cnt = jnp.minimum(pos, win).astype(_F32)
            pooled = acc[POOL_HALO:, :] / cnt - z
            y = jnp.dot(pooled.astype(_BF16), wp_ref[gi], preferred_element_type=_F32)
            b_ref[:, cs] = (y * ps_ref[:, cs]).astype(b_ref.dtype)

    @pl.when(j >= 3)
    def _():
        for lo in range(0, tn, sw):
            gates_ref[:, lo:lo + sw] = jax.nn.sigmoid(zslab(lo, sw)).astype(gates_ref.dtype)


def _inproj(x2, mod, norm_g, w_in, ln_g, ln_b, w_s, b_s_t, w_pool, pool_scale, seq):
    t, d = x2.shape
    d_in = w_in.shape[1]
    tn = ln_g.shape[-1]
    heads, chunk, _ = w_s.shape
    groups, gd, _ = w_pool.shape
    assert pool_scale.shape[-1] == tn and groups * gd == tn and d % tn == 0
    assert d_in == 3 * tn + 2 * d and max(POOL_WINDOWS[:groups]) <= POOL_HALO
    tm = _pick(seq, (512, 256, 128))
    assert tm % chunk == 0 and tm >= POOL_HALO
    kern = functools.partial(_inproj_kernel, tiles_per_seq=seq // tm,
                             heads=heads, chunk=chunk, windows=POOL_WINDOWS[:groups])
    tps = seq // tm
    const2 = lambda i, j: (0, 0)
    const3 = lambda i, j: (0, 0, 0)
    return pl.pallas_call(
        kern,
        grid=(t // tm, d_in // tn),
        in_specs=[
            pl.BlockSpec((tm, d), lambda i, j: (i, 0)),
            pl.BlockSpec((1, 3, d), lambda i, j: (i // tps, 0, 0)),
            pl.BlockSpec((1, d), const2),
            pl.BlockSpec((d, tn), lambda i, j: (0, j)),
            pl.BlockSpec((1, tn), const2),
            pl.BlockSpec((1, tn), const2),
            pl.BlockSpec((heads, chunk, chunk), const3),
            pl.BlockSpec((chunk, heads), const2),
            pl.BlockSpec((groups, gd, gd), const3),
            pl.BlockSpec((1, tn), const2),
        ],
        out_specs=[
            pl.BlockSpec((tm, tn), lambda i, j: (i, 0)),
            pl.BlockSpec((tm, tn), lambda i, j: (i, 0)),
            pl.BlockSpec((tm, tn), lambda i, j: (i, jnp.maximum(j - 3, 0))),
        ],
        out_shape=[
            jax.ShapeDtypeStruct((t, tn), _BF16),
            jax.ShapeDtypeStruct((t, tn), _BF16),
            jax.ShapeDtypeStruct((t, 2 * d), _BF16),
        ],
        scratch_shapes=[
            pltpu.VMEM((tm, d), _BF16),
            pltpu.VMEM((tm, tn), _BF16),
            pltpu.VMEM((tm, tn), _F32),
            pltpu.VMEM((tm, tn), _BF16),
            pltpu.VMEM((POOL_HALO, tn), _F32),
        ],
        compiler_params=_cparams(("arbitrary", "arbitrary")),
        name="mixer_inproj",
    )(x2, mod, norm_g, w_in, ln_g, ln_b, w_s, b_s_t, w_pool, pool_scale)


def _outproj_kernel(a_ref, b_ref, sga_ref, sgb_ref, x_ref, mod_ref, woa_ref, wob_ref, wout_ref,
                    o_ref, m_ref, *, tc):
    d = x_ref.shape[1]
    for c in range(d // tc):
        cs = slice(c * tc, (c + 1) * tc)
        ya = jnp.dot(a_ref[...], woa_ref[:, cs], preferred_element_type=_F32)
        yb = jnp.dot(b_ref[...], wob_ref[:, cs], preferred_element_type=_F32)
        m_ref[:, cs] = (sga_ref[:, cs].astype(_F32) * ya + sgb_ref[:, cs].astype(_F32) * yb).astype(_BF16)
    gate = mod_ref[0][2:3, :]
    for c in range(d // tc):
        cs = slice(c * tc, (c + 1) * tc)
        y = jnp.dot(m_ref[...], wout_ref[:, cs], preferred_element_type=_F32)
        o_ref[:, cs] = x_ref[:, cs] + gate[:, cs] * y


def _outproj(a, b, gates, x2, mod, w_oa, w_ob, w_out, seq):
    t, d = x2.shape
    ds = a.shape[1]
    tm = _pick(seq, (512, 256, 128))
    tc = _pick(d, (512, 256, 128))
    tps = seq // tm
    once = pl.Buffered(1)
    return pl.pallas_call(
        functools.partial(_outproj_kernel, tc=tc),
        grid=(t // tm,),
        in_specs=[
            pl.BlockSpec((tm, ds), lambda i: (i, 0)),
            pl.BlockSpec((tm, ds), lambda i: (i, 0)),
            pl.BlockSpec((tm, d), lambda i: (i, 0)),
            pl.BlockSpec((tm, d), lambda i: (i, 1)),
            pl.BlockSpec((tm, d), lambda i: (i, 0)),
            pl.BlockSpec((1, 3, d), lambda i: (i // tps, 0, 0)),
            pl.BlockSpec((ds, d), lambda i: (0, 0), pipeline_mode=once),
            pl.BlockSpec((ds, d), lambda i: (0, 0), pipeline_mode=once),
            pl.BlockSpec((d, d), lambda i: (0, 0), pipeline_mode=once),
        ],
        out_specs=pl.BlockSpec((tm, d), lambda i: (i, 0)),
        out_shape=jax.ShapeDtypeStruct((t, d), _F32),
        scratch_shapes=[pltpu.VMEM((tm, d), _BF16)],
        compiler_params=_cparams(("arbitrary",)),
        name="mixer_outproj",
    )(a, b, gates, gates, x2, mod, w_oa, w_ob, w_out)


def _swiglu_step(h_ref, wg_ref, wu_ref, wd_ref, o_ref):
    h = h_ref[...]
    gte = jnp.dot(h, wg_ref, preferred_element_type=_F32)
    up = jnp.dot(h, wu_ref, preferred_element_type=_F32)
    act = (_silu(gte) * up).astype(_BF16)
    o_ref[...] += jnp.dot(act, wd_ref, preferred_element_type=_F32)


def _ffn_kernel(x_ref, mod_ref, g_ref, w1_ref, w3_ref, w2_ref, o_ref, h_ref):
    k = pl.program_id(1)
    m = mod_ref[0]

    @pl.when(k == 0)
    def _():
        h_ref[...] = _modnorm(x_ref[...], g_ref[...], m[0:1, :], m[1:2, :]).astype(_BF16)
        o_ref[...] = jnp.zeros_like(o_ref)

    _swiglu_step(h_ref, w1_ref[...], w3_ref[...], w2_ref[...], o_ref)

    @pl.when(k == pl.num_programs(1) - 1)
    def _():
        o_ref[...] = x_ref[...] + m[2:3, :] * o_ref[...]


def _dense_ffn(x2, mod, norm_g, w1, w3, w2, seq):
    t, d = x2.shape
    f = w1.shape[1]
    tm = _pick(seq, (512, 256, 128))
    tf = _pick(f, (512, 256, 128))
    tps = seq // tm
    return pl.pallas_call(
        _ffn_kernel,
        grid=(t // tm, f // tf),
        in_specs=[
            pl.BlockSpec((tm, d), lambda i, k: (i, 0)),
            pl.BlockSpec((1, 3, d), lambda i, k: (i // tps, 0, 0)),
            pl.BlockSpec((1, d), lambda i, k: (0, 0)),
            pl.BlockSpec((d, tf), lambda i, k: (0, k)),
            pl.BlockSpec((d, tf), lambda i, k: (0, k)),
            pl.BlockSpec((tf, d), lambda i, k: (k, 0)),
        ],
        out_specs=pl.BlockSpec((tm, d), lambda i, k: (i, 0)),
        out_shape=jax.ShapeDtypeStruct((t, d), _F32),
        scratch_shapes=[pltpu.VMEM((tm, d), _BF16)],
        compiler_params=_cparams(("arbitrary", "arbitrary")),
        name="dense_swiglu",
    )(x2, mod, norm_g, w1, w3, w2)


def _route_kernel(x_ref, mod_ref, g_ref, wrt_ref, h_ref, route_ref, cnt_ref, run_ref):
    i = pl.program_id(0)
    tm = x_ref.shape[0]
    ne = wrt_ref.shape[0]

    @pl.when(i == 0)
    def _():
        run_ref[...] = jnp.zeros_like(run_ref)

    m = mod_ref[0]
    h = _modnorm(x_ref[...], g_ref[...], m[0:1, :], m[1:2, :])
    h_ref[...] = h
    lt = lax.dot_general(wrt_ref[...], h.astype(_BF16), (((1,), (1,)), ((), ())),
                         preferred_element_type=_F32)
    eid = lax.broadcasted_iota(jnp.int32, (ne, tm), 0).astype(_F32)
    m1 = jnp.max(lt, axis=0, keepdims=True)
    i1 = jnp.min(jnp.where(lt == m1, eid, float(ne)), axis=0, keepdims=True)
    l2 = jnp.where(eid == i1, -jnp.inf, lt)
    m2 = jnp.max(l2, axis=0, keepdims=True)
    i2 = jnp.min(jnp.where(l2 == m2, eid, float(ne)), axis=0, keepdims=True)
    e2 = jnp.exp(m2 - m1)
    g1 = 1.0 / (1.0 + e2)
    g2 = e2 / (1.0 + e2)
    oh1 = eid == i1
    oh2 = eid == i2
    both = jnp.where(oh1 | oh2, 1.0, 0.0).astype(_BF16)
    r = lax.broadcasted_iota(jnp.int32, (tm, tm), 0)
    c = lax.broadcasted_iota(jnp.int32, (tm, tm), 1)
    upper = jnp.where(r <= c, 1.0, 0.0).astype(_BF16)
    pref = jnp.dot(both, upper, preferred_element_type=_F32)
    base = run_ref[:, 0:1] + pref - 1.0
    rank1 = jnp.sum(jnp.where(oh1, base, 0.0), axis=0, keepdims=True)
    rank2 = jnp.sum(jnp.where(oh2, base, 0.0), axis=0, keepdims=True)
    run_new = run_ref[:, 0:1] + pref[:, tm - 1:tm]
    run_ref[...] = jnp.broadcast_to(run_new, run_ref.shape)
    cnt_ref[...] = jnp.broadcast_to(run_new, cnt_ref.shape)
    route_ref[0:1, :] = i1
    route_ref[1:2, :] = i2
    route_ref[2:3, :] = rank1
    route_ref[3:4, :] = rank2
    route_ref[4:5, :] = g1
    route_ref[5:6, :] = g2
    route_ref[6:8, :] = jnp.zeros((2, tm), _F32)


def _route(x2, mod, norm_g, w_router_t, seq):
    t, d = x2.shape
    ne = w_router_t.shape[0]
    tm = _pick(seq, (512, 256, 128))
    tps = seq // tm
    return pl.pallas_call(
        _route_kernel,
        grid=(t // tm,),
        in_specs=[
            pl.BlockSpec((tm, d), lambda i: (i, 0)),
            pl.BlockSpec((1, 3, d), lambda i: (i // tps, 0, 0)),
            pl.BlockSpec((1, d), lambda i: (0, 0)),
            pl.BlockSpec((ne, d), lambda i: (0, 0)),
        ],
        out_specs=[
            pl.BlockSpec((tm, d), lambda i: (i, 0)),
            pl.BlockSpec((8, tm), lambda i: (0, i)),
            pl.BlockSpec((ne, 128), lambda i: (0, 0)),
        ],
        out_shape=[
            jax.ShapeDtypeStruct((t, d), _F32),
            jax.ShapeDtypeStruct((8, t), _F32),
            jax.ShapeDtypeStruct((ne, 128), _F32),
        ],
        scratch_shapes=[pltpu.VMEM((ne, 128), _F32)],
        compiler_params=_cparams(("arbitrary",)),
        name="moe_route",
    )(x2, mod, norm_g, w_router_t)


def _row_copy(src_ref, src_row, dst_ref, dst_row, sem):
    return pltpu.make_async_copy(src_ref.at[pl.ds(src_row, 1)], dst_ref.at[pl.ds(dst_row, 1)], sem)


def _dispatch_kernel(dest_ref, h_ref, xs_ref, sem):
    tm = h_ref.shape[0]

    def issue(r, carry):
        for kk in range(TOP_K):
            _row_copy(h_ref, r, xs_ref, dest_ref[0, 0, kk * tm + r], sem).start()
        return carry

    lax.fori_loop(0, tm, issue, 0, unroll=DMA_ISSUE_UNROLL)
    for kk in range(TOP_K):
        pltpu.make_async_copy(h_ref, xs_ref.at[pl.ds(0, tm)], sem).wait()


def _dispatch(h, dest_tiles, n_rows):
    t, d = h.shape
    tm = dest_tiles.shape[-1] // TOP_K
    return pl.pallas_call(
        _dispatch_kernel,
        grid=(t // tm,),
        in_specs=[
            pl.BlockSpec((1, 1, TOP_K * tm), lambda i: (i, 0, 0), memory_space=pltpu.SMEM),
            pl.BlockSpec((tm, d), lambda i: (i, 0)),
        ],
        out_specs=pl.BlockSpec(memory_space=pl.ANY),
        out_shape=jax.ShapeDtypeStruct((n_rows, d), _F32),
        scratch_shapes=[pltpu.SemaphoreType.DMA(())],
        compiler_params=_cparams(("arbitrary",)),
        name="moe_dispatch",
    )(dest_tiles, h)


def _combine_kernel(dest_ref, ys_ref, gates_ref, x_ref, mod_ref, fg_ref, o_ref, buf_ref, sem, *, final_norm):
    tm = x_ref.shape[0]

    def issue(r, carry):
        for kk in range(TOP_K):
            _row_copy(ys_ref, dest_ref[0, 0, kk * tm + r], buf_ref.at[kk], r, sem).start()
        return carry

    lax.fori_loop(0, tm, issue, 0, unroll=DMA_ISSUE_UNROLL)
    for kk in range(TOP_K):
        pltpu.make_async_copy(ys_ref.at[pl.ds(0, tm)], buf_ref.at[kk], sem).wait()
    y = gates_ref[:, 0:1] * buf_ref[0]
    for kk in range(1, TOP_K):
        y = y + gates_ref[:, kk:kk + 1] * buf_ref[kk]
    xn = x_ref[...] + mod_ref[0][2:3, :] * y
    if final_norm:
        xn = xn * lax.rsqrt(jnp.mean(xn * xn, axis=-1, keepdims=True) + EPS) * fg_ref[...]
    o_ref[...] = xn


def _combine(ys, dest_tiles, gates, x2, mod, final_g, seq, final_norm):
    t, d = x2.shape
    tm = dest_tiles.shape[-1] // TOP_K
    tps = seq // tm
    return pl.pallas_call(
        functools.partial(_combine_kernel, final_norm=final_norm),
        grid=(t // tm,),
        in_specs=[
            pl.BlockSpec((1, 1, TOP_K * tm), lambda i: (i, 0, 0), memory_space=pltpu.SMEM),
            pl.BlockSpec(memory_space=pl.ANY),
            pl.BlockSpec((tm, TOP_K), lambda i: (i, 0)),
            pl.BlockSpec((tm, d), lambda i: (i, 0)),
            pl.BlockSpec((1, 3, d), lambda i: (i // tps, 0, 0)),
            pl.BlockSpec((1, d), lambda i: (0, 0)),
        ],
        out_specs=pl.BlockSpec((tm, d), lambda i: (i, 0)),
        out_shape=jax.ShapeDtypeStruct((t, d), _F32),
        scratch_shapes=[pltpu.VMEM((TOP_K, tm, d), _F32), pltpu.SemaphoreType.DMA(())],
        compiler_params=_cparams(("arbitrary",)),
        name="moe_combine",
    )(dest_tiles, ys, gates, x2, mod, final_g)


def _experts_kernel(te_ref, nv_ref, lim_ref, xs_ref, wg_ref, wu_ref, wd_ref, o_ref, h_ref):
    i = pl.program_id(0)
    k = pl.program_id(1)
    tm = xs_ref.shape[0]

    @pl.when(i < nv_ref[0])
    def _():
        @pl.when(k == 0)
        def _():
            rows = i * tm + lax.broadcasted_iota(jnp.int32, (tm, 1), 0)
            h_ref[...] = jnp.where(rows < lim_ref[i], xs_ref[...], 0.0).astype(_BF16)
            o_ref[...] = jnp.zeros_like(o_ref)

        _swiglu_step(h_ref, wg_ref[0], wu_ref[0], wd_ref[0], o_ref)


def _experts(xs, tile_expert, n_valid, row_limit, w_gate, w_up, w_down, tm):
    r, d = xs.shape
    ne, _, f = w_gate.shape
    tf = _pick(f, (1024, 512, 256, 128))
    nk = f // tf
    n_tiles = r // tm

    def row_map(i, k, te, nv, lim):
        return (jnp.minimum(i, nv[0] - 1), 0)

    def kk(i, k, nv):
        return jnp.where(i < nv[0], k, nk - 1)

    return pl.pallas_call(
        _experts_kernel,
        grid_spec=pltpu.PrefetchScalarGridSpec(
            num_scalar_prefetch=3,
            grid=(n_tiles, nk),
            in_specs=[
                pl.BlockSpec((tm, d), row_map),
                pl.BlockSpec((1, d, tf), lambda i, k, te, nv, lim: (te[i], 0, kk(i, k, nv))),
                pl.BlockSpec((1, d, tf), lambda i, k, te, nv, lim: (te[i], 0, kk(i, k, nv))),
                pl.BlockSpec((1, tf, d), lambda i, k, te, nv, lim: (te[i], kk(i, k, nv), 0)),
            ],
            out_specs=pl.BlockSpec((tm, d), row_map),
            scratch_shapes=[pltpu.VMEM((tm, d), _BF16)],
        ),
        out_shape=jax.ShapeDtypeStruct((r, d), _F32),
        compiler_params=_cparams(("arbitrary", "arbitrary")),
        name="moe_experts",
    )(tile_expert, n_valid, row_limit, xs, w_gate, w_up, w_down)


def _moe_layer(x2, mod, norm_g, w_router, w_gate, w_up, w_down, final_g, seq, final_norm):
    t, d = x2.shape
    ne = w_router.shape[1]
    h, route, cnt = _route(x2, mod, norm_g, w_router.T.astype(_BF16), seq)
    tm = _pick(seq, (512, 256, 128))
    counts = cnt[:, 0].astype(jnp.int32)
    padded = (counts + tm - 1) // tm * tm
    pend = jnp.cumsum(padded)
    pstart = pend - padded
    idx = route[0:TOP_K].astype(jnp.int32)
    rank = route[TOP_K:2 * TOP_K].astype(jnp.int32)
    dest = rank
    for e in range(ne):
        dest = dest + jnp.where(idx == e, pstart[e], 0)
    gates = route[2 * TOP_K:3 * TOP_K].T
    n_tiles = (TOP_K * t) // tm + ne
    dest_tiles = dest.reshape(TOP_K, t // tm, tm).transpose(1, 0, 2).reshape(t // tm, 1, TOP_K * tm)
    tile_start = jnp.arange(n_tiles, dtype=jnp.int32) * tm
    tile_expert = jnp.minimum(jnp.sum(pend[None, :] <= tile_start[:, None], axis=1), ne - 1).astype(jnp.int32)
    n_valid = (pend[-1] // tm).astype(jnp.int32).reshape(1)
    row_limit = jnp.zeros((n_tiles,), jnp.int32)
    for e in range(ne):
        row_limit = row_limit + jnp.where(tile_expert == e, pstart[e] + counts[e], 0)
    xs = _dispatch(h, dest_tiles, n_tiles * tm)
    ys = _experts(xs, tile_expert, n_valid, row_limit, w_gate, w_up, w_down, tm)
    return _combine(ys, dest_tiles, gates, x2, mod, final_g, seq, final_norm)


def _final_norm_kernel(x_ref, g_ref, o_ref):
    x = x_ref[...]
    o_ref[...] = x * lax.rsqrt(jnp.mean(x * x, axis=-1, keepdims=True) + EPS) * g_ref[...]


def _final_norm(x2, fg):
    t, d = x2.shape
    tm = _pick(t, (512, 256, 128))
    return pl.pallas_call(
        _final_norm_kernel,
        grid=(t // tm,),
        in_specs=[pl.BlockSpec((tm, d), lambda i: (i, 0)), pl.BlockSpec((1, d), lambda i: (0, 0))],
        out_specs=pl.BlockSpec((tm, d), lambda i: (i, 0)),
        out_shape=jax.ShapeDtypeStruct((t, d), _F32),
        compiler_params=_cparams(("arbitrary",)),
        name="final_norm",
    )(x2, fg)


def kernel(x, c, w_ada, b_ada, norm_g, w_in, ln_g, ln_b, w_s, b_s, w_pool, pool_scale, w_oa, w_ob, w_out,
           ffn_w1, ffn_w3, ffn_w2, w_router, moe_w_gate, moe_w_up, moe_w_down, final_g):
    bsz, seq, d = x.shape
    depth = w_in.shape[0]
    t = bsz * seq
    x2 = x.reshape(t, d)
    mods = _adaln_all(c, w_ada.reshape(depth * 2, d, 3 * d), b_ada.reshape(depth * 2, 1, 3 * d))
    mods = mods.reshape(depth * 2, bsz, 3, d)
    fg = final_g.reshape(1, d)
    bf = lambda w: w.astype(_BF16)
    for i in range(depth):
        last = i == depth - 1
        a, b, gates = _inproj(x2, mods[2 * i], norm_g[i, 0].reshape(1, d), bf(w_in[i]),
                              ln_g[i].reshape(1, -1), ln_b[i].reshape(1, -1), w_s[i], b_s[i].T,
                              bf(w_pool[i]), pool_scale[i].reshape(1, -1), seq)
        x2 = _outproj(a, b, gates, x2, mods[2 * i], bf(w_oa[i]), bf(w_ob[i]), bf(w_out[i]), seq)
        j = i // 2
        ng = norm_g[i, 1].reshape(1, d)
        if i % 2 == 0:
            x2 = _dense_ffn(x2, mods[2 * i + 1], ng, bf(ffn_w1[j]), bf(ffn_w3[j]), bf(ffn_w2[j]), seq)
            if last:
                x2 = _final_norm(x2, fg)
        else:
            x2 = _moe_layer(x2, mods[2 * i + 1], ng, w_router[j], bf(moe_w_gate[j]), bf(moe_w_up[j]),
                            bf(moe_w_down[j]), fg, seq, final_norm=last)
    return x2.reshape(bsz, seq, d)
```

```python
import functools

import jax
import jax.numpy as jnp
from jax import lax
from jax.experimental import pallas as pl
from jax.experimental.pallas import tpu as pltpu

EPS = 1e-6
POOL_WINDOWS = (2, 4, 8, 16)
TOP_K = 2
POOL_HALO = 16
V7X_MXU_COLS = 256
V7X_VMEM_LIMIT_BYTES = 56 * 1024 * 1024
BF16_SUBLANE_TILE = 16
DMA_ISSUE_UNROLL = 8

_BF16 = jnp.bfloat16
_F32 = jnp.float32


def _cparams(sem):
    return pltpu.CompilerParams(dimension_semantics=sem, vmem_limit_bytes=V7X_VMEM_LIMIT_BYTES)


def _gelu_tanh(x):
    return 0.5 * x * (1.0 + jnp.tanh(0.7978845608028654 * (x + 0.044715 * (x * x * x))))


def _silu(x):
    return x * jax.nn.sigmoid(x)


def _modnorm(x, g, shift, scale):
    y = x * lax.rsqrt(jnp.mean(x * x, axis=-1, keepdims=True) + EPS)
    return (y * g) * (1.0 + scale) + shift


def _pick(n, pref):
    for t in pref:
        if n % t == 0:
            return t
    return n


def _ada_kernel(c_ref, w_ref, b_ref, o_ref):
    sc = _silu(c_ref[...]).astype(_BF16)
    o_ref[0] = jnp.dot(sc, w_ref[0].astype(_BF16), preferred_element_type=_F32) + b_ref[0]


def _adaln_all(c, w_ada, b_ada):
    ns, d, d3 = w_ada.shape
    b = c.shape[0]
    tn = _pick(d3, (768, 512, 256, 128))
    return pl.pallas_call(
        _ada_kernel,
        grid=(ns, d3 // tn),
        in_specs=[
            pl.BlockSpec((b, d), lambda s, j: (0, 0)),
            pl.BlockSpec((1, d, tn), lambda s, j: (s, 0, j)),
            pl.BlockSpec((1, 1, tn), lambda s, j: (s, 0, j)),
        ],
        out_specs=pl.BlockSpec((1, b, tn), lambda s, j: (s, 0, j)),
        out_shape=jax.ShapeDtypeStruct((ns, b, d3), _F32),
        compiler_params=_cparams(("arbitrary", "arbitrary")),
        name="adaln_mod",
    )(c, w_ada, b_ada)


def _inproj_kernel(x_ref, mod_ref, g_ref, w_ref, lng_ref, lnb_ref, ws_ref, bst_ref, wp_ref, ps_ref,
                   a_ref, b_ref, gates_ref,
                   h_ref, u_ref, v_ref, p_ref, vn_ref, halo_ref,
                   *, tiles_per_seq, heads, chunk, windows):
    i = pl.program_id(0)
    j = pl.program_id(1)
    tm = x_ref.shape[0]
    tn = w_ref.shape[1]
    sw = min(V7X_MXU_COLS, tn)

    def zslab(lo, width):
        return jnp.dot(h_ref[...], w_ref[:, lo:lo + width], preferred_element_type=_F32)

    def sgu_epilogue():
        rsum = jnp.zeros((tm, 1), _F32)
        for lo in range(0, tn, sw):
            rsum = rsum + jnp.sum(v_ref[:, lo:lo + sw], axis=-1, keepdims=True)
        mu = rsum * (1.0 / tn)
        vsum = jnp.zeros((tm, 1), _F32)
        for lo in range(0, tn, sw):
            vc = v_ref[:, lo:lo + sw] - mu
            vsum = vsum + jnp.sum(vc * vc, axis=-1, keepdims=True)
        rstd = lax.rsqrt(vsum * (1.0 / tn) + EPS)
        for lo in range(0, tn, sw):
            cs = slice(lo, lo + sw)
            vn_ref[:, cs] = ((v_ref[:, cs] - mu) * rstd * lng_ref[:, cs] + lnb_ref[:, cs]).astype(_BF16)
        hd = tn // heads
        row = lax.broadcasted_iota(jnp.int32, (chunk, chunk), 0)
        col = lax.broadcasted_iota(jnp.int32, (chunk, chunk), 1)
        causal = col <= row
        for hh in range(heads):
            wm = jnp.where(causal, ws_ref[hh], 0.0).astype(_BF16)
            bias = bst_ref[:, hh:hh + 1]
            for cc in range(tm // chunk):
                rs = slice(cc * chunk, (cc + 1) * chunk)
                cs = slice(hh * hd, (hh + 1) * hd)
                mixed = jnp.dot(wm, vn_ref[rs, cs], preferred_element_type=_F32) + bias
                a_ref[rs, cs] = (u_ref[rs, cs].astype(_F32) * mixed).astype(a_ref.dtype)

    def pool_epilogue():
        first = (i % tiles_per_seq) == 0
        pos = (i % tiles_per_seq) * tm + lax.broadcasted_iota(jnp.int32, (tm, 1), 0) + 1
        gd = tn // len(windows)
        for gi, win in enumerate(windows):
            cs = slice(gi * gd, (gi + 1) * gd)
            z = p_ref[:, cs]
            hist = jnp.where(first, 0.0, halo_ref[:, cs])
            acc = jnp.concatenate([hist, z], axis=0)
            halo_ref[:, cs] = z[tm - POOL_HALO:, :]
            step = 1
            while step < win:
                acc = acc + pltpu.roll(acc, step, 0)
                step *= 2
            cnt = jnp.minimum(pos, win).astype(_F32)
            pooled = acc[POOL_HALO:, :] / cnt - z
            y = jnp.dot(pooled.astype(_BF16), wp_ref[gi], preferred_element_type=_F32)
            b_ref[:, cs] = (y * ps_ref[:, cs]).astype(b_ref.dtype)

    @pl.when(j == 0)
    def _():
        m = mod_ref[0]
        h_ref[...] = _modnorm(x_ref[...], g_ref[...], m[0:1, :], m[1:2, :]).astype(_BF16)
        for lo in range(0, tn, sw):
            u_ref[:, lo:lo + sw] = _gelu_tanh(zslab(lo, sw)).astype(u_ref.dtype)

    @pl.when(j == 1)
    def _():
        for lo in range(0, tn, sw):
            v_ref[:, lo:lo + sw] = _gelu_tanh(zslab(lo, sw))
        sgu_epilogue()

    @pl.when(j == 2)
    def _():
        for lo in range(0, tn, sw):
            p_ref[:, lo:lo + sw] = zslab(lo, sw)
        pool_epilogue()

    @pl.when(j >= 3)
    def _():
        for lo in range(0, tn, sw):
            gates_ref[:, lo:lo + sw] = jax.nn.sigmoid(zslab(lo, sw)).astype(gates_ref.dtype)


def _inproj(x2, mod, norm_g, w_in, ln_g, ln_b, w_s, b_s_t, w_pool, pool_scale, seq):
    t, d = x2.shape
    d_in = w_in.shape[1]
    tn = ln_g.shape[-1]
    heads, chunk, _ = w_s.shape
    groups, gd, _ = w_pool.shape
    assert pool_scale.shape[-1] == tn and groups * gd == tn and d % tn == 0
    assert d_in == 3 * tn + 2 * d and max(POOL_WINDOWS[:groups]) <= POOL_HALO
    tm = _pick(seq, (512, 256, 128))
    assert tm % chunk == 0 and tm >= POOL_HALO
    kern = functools.partial(_inproj_kernel, tiles_per_seq=seq // tm, heads=heads, chunk=chunk,
                             windows=POOL_WINDOWS[:groups])
    tps = seq // tm
    const2 = lambda i, j: (0, 0)
    const3 = lambda i, j: (0, 0, 0)
    return pl.pallas_call(
        kern,
        grid=(t // tm, d_in // tn),
        in_specs=[
            pl.BlockSpec((tm, d), lambda i, j: (i, 0)),
            pl.BlockSpec((1, 3, d), lambda i, j: (i // tps, 0, 0)),
            pl.BlockSpec((1, d), const2),
            pl.BlockSpec((d, tn), lambda i, j: (0, j)),
            pl.BlockSpec((1, tn), const2),
            pl.BlockSpec((1, tn), const2),
            pl.BlockSpec((heads, chunk, chunk), const3),
            pl.BlockSpec((chunk, heads), const2),
            pl.BlockSpec((groups, gd, gd), const3),
            pl.BlockSpec((1, tn), const2),
        ],
        out_specs=[
            pl.BlockSpec((tm, tn), lambda i, j: (i, 0)),
            pl.BlockSpec((tm, tn), lambda i, j: (i, 0)),
            pl.BlockSpec((tm, tn), lambda i, j: (i, jnp.maximum(j - 3, 0))),
        ],
        out_shape=[
            jax.ShapeDtypeStruct((t, tn), _BF16),
            jax.ShapeDtypeStruct((t, tn), _BF16),
            jax.ShapeDtypeStruct((t, 2 * d), _BF16),
        ],
        scratch_shapes=[
            pltpu.VMEM((tm, d), _BF16),
            pltpu.VMEM((tm, tn), _BF16),
            pltpu.VMEM((tm, tn), _F32),
            pltpu.VMEM((tm, tn), _F32),
            pltpu.VMEM((tm, tn), _BF16),
            pltpu.VMEM((POOL_HALO, tn), _F32),
        ],
        compiler_params=_cparams(("arbitrary", "arbitrary")),
        name="mixer_inproj",
    )(x2, mod, norm_g, w_in, ln_g, ln_b, w_s, b_s_t, w_pool, pool_scale)


def _outproj_kernel(a_ref, b_ref, sga_ref, sgb_ref, x_ref, mod_ref, woa_ref, wob_ref, wout_ref,
                    o_ref, m_ref, *, tc):
    d = x_ref.shape[1]
    for c in range(d // tc):
        cs = slice(c * tc, (c + 1) * tc)
        ya = jnp.dot(a_ref[...], woa_ref[:, cs], preferred_element_type=_F32)
        yb = jnp.dot(b_ref[...], wob_ref[:, cs], preferred_element_type=_F32)
        m_ref[:, cs] = (sga_ref[:, cs].astype(_F32) * ya + sgb_ref[:, cs].astype(_F32) * yb).astype(_BF16)
    gate = mod_ref[0][2:3, :]
    for c in range(d // tc):
        cs = slice(c * tc, (c + 1) * tc)
        y = jnp.dot(m_ref[...], wout_ref[:, cs], preferred_element_type=_F32)
        o_ref[:, cs] = x_ref[:, cs] + gate[:, cs] * y


def _outproj(a, b, gates, x2, mod, w_oa, w_ob, w_out, seq):
    t, d = x2.shape
    ds = a.shape[1]
    tm = _pick(seq, (512, 256, 128))
    tc = _pick(d, (512, 256, 128))
    tps = seq // tm
    once = pl.Buffered(1)
    return pl.pallas_call(
        functools.partial(_outproj_kernel, tc=tc),
        grid=(t // tm,),
        in_specs=[
            pl.BlockSpec((tm, ds), lambda i: (i, 0)),
            pl.BlockSpec((tm, ds), lambda i: (i, 0)),
            pl.BlockSpec((tm, d), lambda i: (i, 0)),
            pl.BlockSpec((tm, d), lambda i: (i, 1)),
            pl.BlockSpec((tm, d), lambda i: (i, 0)),
            pl.BlockSpec((1, 3, d), lambda i: (i // tps, 0, 0)),
            pl.BlockSpec((ds, d), lambda i: (0, 0), pipeline_mode=once),
            pl.BlockSpec((ds, d), lambda i: (0, 0), pipeline_mode=once),
            pl.BlockSpec((d, d), lambda i: (0, 0), pipeline_mode=once),
        ],
        out_specs=pl.BlockSpec((tm, d), lambda i: (i, 0)),
        out_shape=jax.ShapeDtypeStruct((t, d), _F32),
        scratch_shapes=[pltpu.VMEM((tm, d), _BF16)],
        compiler_params=_cparams(("arbitrary",)),
        name="mixer_outproj",
    )(a, b, gates, gates, x2, mod, w_oa, w_ob, w_out)


def _swiglu_tile(h, wg, wu, wd):
    gte = jnp.dot(h, wg, preferred_element_type=_F32)
    up = jnp.dot(h, wu, preferred_element_type=_F32)
    act = (_silu(gte) * up).astype(_BF16)
    return jnp.dot(act, wd, preferred_element_type=_F32)


def _ffn_kernel(*refs, n_cast, cast_k, norm_k):
    x_ref, mod_ref, mcur_ref, g_ref, w1_ref, w3_ref, w2_ref = refs[:7]
    cast_in = refs[7:7 + n_cast]
    o_ref = refs[7 + n_cast]
    cast_out = refs[8 + n_cast:8 + 2 * n_cast]
    h_ref, hn_ref = refs[8 + 2 * n_cast:]
    i = pl.program_id(0)
    k = pl.program_id(1)

    def norm_next():
        m = mod_ref[0]
        hn_ref[...] = _modnorm(x_ref[...], g_ref[...], m[0:1, :], m[1:2, :]).astype(_BF16)

    def gated_tile(lhs_ref):
        return mcur_ref[0][2:3, :] * _swiglu_tile(lhs_ref[...], w1_ref[...], w3_ref[...], w2_ref[...])

    def cast_blocks():
        for src, dst in zip(cast_in, cast_out):
            dst[...] = src[...].astype(dst.dtype)

    @pl.when((i == 0) & (k == 0))
    def _():
        norm_next()

    @pl.when(k == 0)
    def _():
        o_ref[...] = x_ref[...] + gated_tile(hn_ref)
        h_ref[...] = hn_ref[...]
        if cast_k > 0:
            cast_blocks()

    @pl.when((k > 0) & (k < cast_k))
    def _():
        o_ref[...] += gated_tile(h_ref)
        cast_blocks()

    @pl.when(k == norm_k)
    def _():
        o_ref[...] += gated_tile(h_ref)
        norm_next()

    @pl.when((k >= max(cast_k, 1)) & (k != norm_k))
    def _():
        o_ref[...] += gated_tile(h_ref)


def _cast_plan(arrays, n_i, nk):
    for c in range(max(nk - 2, 0), 0, -1):
        if all(a.shape[0] % (n_i * c) == 0 and (a.shape[0] // (n_i * c)) % BF16_SUBLANE_TILE == 0 for a in arrays):
            return c
    return None


def _dense_ffn(x2, mod, norm_g, w1, w3, w2, seq, cast_srcs=()):
    t, d = x2.shape
    f = w1.shape[1]
    tm = _pick(seq, (512, 256, 128))
    tf = _pick(f, (512, 256, 128))
    tps = seq // tm
    n_i, nk = t // tm, f // tf
    srcs2d = [s.reshape(-1, s.shape[-1]) for s in cast_srcs]
    cast_k = _cast_plan(srcs2d, n_i, nk) if srcs2d else None
    if cast_k is None:
        srcs2d, cast_k = [], 0
    norm_k = nk - 1 if cast_k == 0 else max(cast_k, nk - 2)
    assert nk >= 2 and norm_k >= cast_k

    def x_tile(i, k):
        return jnp.minimum(i + (k >= norm_k).astype(jnp.int32), n_i - 1)

    def cast_block(i, k):
        return (i * cast_k + jnp.minimum(k, cast_k - 1), 0)

    cast_specs = [pl.BlockSpec((s.shape[0] // (n_i * cast_k), s.shape[1]), cast_block) for s in srcs2d]
    outs = pl.pallas_call(
        functools.partial(_ffn_kernel, n_cast=len(srcs2d), cast_k=cast_k, norm_k=norm_k),
        grid=(n_i, nk),
        in_specs=[
            pl.BlockSpec((tm, d), lambda i, k: (x_tile(i, k), 0)),
            pl.BlockSpec((1, 3, d), lambda i, k: (x_tile(i, k) // tps, 0, 0)),
            pl.BlockSpec((1, 3, d), lambda i, k: (i // tps, 0, 0)),
            pl.BlockSpec((1, d), lambda i, k: (0, 0)),
            pl.BlockSpec((d, tf), lambda i, k: (0, k)),
            pl.BlockSpec((d, tf), lambda i, k: (0, k)),
            pl.BlockSpec((tf, d), lambda i, k: (k, 0)),
        ] + cast_specs,
        out_specs=[pl.BlockSpec((tm, d), lambda i, k: (i, 0))] + cast_specs,
        out_shape=[jax.ShapeDtypeStruct((t, d), _F32)] + [jax.ShapeDtypeStruct(s.shape, _BF16) for s in srcs2d],
        scratch_shapes=[pltpu.VMEM((tm, d), _BF16), pltpu.VMEM((tm, d), _BF16)],
        compiler_params=_cparams(("arbitrary", "arbitrary")),
        name="dense_swiglu",
    )(x2, mod, mod, norm_g, w1, w3, w2, *srcs2d)
    if not srcs2d:
        return outs[0], None
    return outs[0], [o.reshape(s.shape) for o, s in zip(outs[1:], cast_srcs)]


def _route_kernel(x_ref, mod_ref, g_ref, wrt_ref, h_ref, route_ref, cnt_ref, run_ref):
    i = pl.program_id(0)
    tm = x_ref.shape[0]
    ne = wrt_ref.shape[0]

    @pl.when(i == 0)
    def _():
        run_ref[...] = jnp.zeros_like(run_ref)

    m = mod_ref[0]
    h = _modnorm(x_ref[...], g_ref[...], m[0:1, :], m[1:2, :])
    h_ref[...] = h
    lt = lax.dot_general(wrt_ref[...], h.astype(_BF16), (((1,), (1,)), ((), ())),
                         preferred_element_type=_F32)
    eid = lax.broadcasted_iota(jnp.int32, (ne, tm), 0).astype(_F32)
    m1 = jnp.max(lt, axis=0, keepdims=True)
    i1 = jnp.min(jnp.where(lt == m1, eid, float(ne)), axis=0, keepdims=True)
    l2 = jnp.where(eid == i1, -jnp.inf, lt)
    m2 = jnp.max(l2, axis=0, keepdims=True)
    i2 = jnp.min(jnp.where(l2 == m2, eid, float(ne)), axis=0, keepdims=True)
    e2 = jnp.exp(m2 - m1)
    g1 = 1.0 / (1.0 + e2)
    g2 = e2 / (1.0 + e2)
    oh1 = eid == i1
    oh2 = eid == i2
    both = jnp.where(oh1 | oh2, 1.0, 0.0).astype(_BF16)
    r = lax.broadcasted_iota(jnp.int32, (tm, tm), 0)
    c = lax.broadcasted_iota(jnp.int32, (tm, tm), 1)
    upper = jnp.where(r <= c, 1.0, 0.0).astype(_BF16)
    pref = jnp.dot(both, upper, preferred_element_type=_F32)
    base = run_ref[:, 0:1] + pref - 1.0
    rank1 = jnp.sum(jnp.where(oh1, base, 0.0), axis=0, keepdims=True)
    rank2 = jnp.sum(jnp.where(oh2, base, 0.0), axis=0, keepdims=True)
    run_new = run_ref[:, 0:1] + pref[:, tm - 1:tm]
    run_ref[...] = jnp.broadcast_to(run_new, run_ref.shape)
    cnt_ref[...] = jnp.broadcast_to(run_new, cnt_ref.shape)
    route_ref[0:1, :] = i1
    route_ref[1:2, :] = i2
    route_ref[2:3, :] = rank1
    route_ref[3:4, :] = rank2
    route_ref[4:5, :] = g1
    route_ref[5:6, :] = g2
    route_ref[6:8, :] = jnp.zeros((2, tm), _F32)


def _route(x2, mod, norm_g, w_router_t, seq):
    t, d = x2.shape
    ne = w_router_t.shape[0]
    tm = _pick(seq, (512, 256, 128))
    tps = seq // tm
    return pl.pallas_call(
        _route_kernel,
        grid=(t // tm,),
        in_specs=[
            pl.BlockSpec((tm, d), lambda i: (i, 0)),
            pl.BlockSpec((1, 3, d), lambda i: (i // tps, 0, 0)),
            pl.BlockSpec((1, d), lambda i: (0, 0)),
            pl.BlockSpec((ne, d), lambda i: (0, 0)),
        ],
        out_specs=[
            pl.BlockSpec((tm, d), lambda i: (i, 0)),
            pl.BlockSpec((8, tm), lambda i: (0, i)),
            pl.BlockSpec((ne, 128), lambda i: (0, 0)),
        ],
        out_shape=[
            jax.ShapeDtypeStruct((t, d), _F32),
            jax.ShapeDtypeStruct((8, t), _F32),
            jax.ShapeDtypeStruct((ne, 128), _F32),
        ],
        scratch_shapes=[pltpu.VMEM((ne, 128), _F32)],
        compiler_params=_cparams(("arbitrary",)),
        name="moe_route",
    )(x2, mod, norm_g, w_router_t)


def _row_copy(src_ref, src_row, dst_ref, dst_row, sem):
    return pltpu.make_async_copy(src_ref.at[pl.ds(src_row, 1)], dst_ref.at[pl.ds(dst_row, 1)], sem)


def _dispatch_kernel(dest_ref, h_ref, xs_ref, sem):
    tm = h_ref.shape[0]

    def issue(r, carry):
        for kk in range(TOP_K):
            _row_copy(h_ref, r, xs_ref, dest_ref[0, 0, kk * tm + r], sem).start()
        return carry

    lax.fori_loop(0, tm, issue, 0, unroll=DMA_ISSUE_UNROLL)
    for kk in range(TOP_K):
        pltpu.make_async_copy(h_ref, xs_ref.at[pl.ds(0, tm)], sem).wait()


def _dispatch(h, dest_tiles, n_rows):
    t, d = h.shape
    tm = dest_tiles.shape[-1] // TOP_K
    return pl.pallas_call(
        _dispatch_kernel,
        grid=(t // tm,),
        in_specs=[
            pl.BlockSpec((1, 1, TOP_K * tm), lambda i: (i, 0, 0), memory_space=pltpu.SMEM),
            pl.BlockSpec((tm, d), lambda i: (i, 0)),
        ],
        out_specs=pl.BlockSpec(memory_space=pl.ANY),
        out_shape=jax.ShapeDtypeStruct((n_rows, d), _F32),
        scratch_shapes=[pltpu.SemaphoreType.DMA(())],
        compiler_params=_cparams(("arbitrary",)),
        name="moe_dispatch",
    )(dest_tiles, h)


def _combine_kernel(dest_ref, ys_ref, gates_ref, x_ref, mod_ref, fg_ref, o_ref, buf_ref, sem, *, final_norm):
    tm = x_ref.shape[0]

    def issue(r, carry):
        for kk in range(TOP_K):
            _row_copy(ys_ref, dest_ref[0, 0, kk * tm + r], buf_ref.at[kk], r, sem).start()
        return carry

    lax.fori_loop(0, tm, issue, 0, unroll=DMA_ISSUE_UNROLL)
    for kk in range(TOP_K):
        pltpu.make_async_copy(ys_ref.at[pl.ds(0, tm)], buf_ref.at[kk], sem).wait()
    y = gates_ref[:, 0:1] * buf_ref[0]
    for kk in range(1, TOP_K):
        y = y + gates_ref[:, kk:kk + 1] * buf_ref[kk]
    xn = x_ref[...] + mod_ref[0][2:3, :] * y
    if final_norm:
        xn = xn * lax.rsqrt(jnp.mean(xn * xn, axis=-1, keepdims=True) + EPS) * fg_ref[...]
    o_ref[...] = xn


def _combine(ys, dest_tiles, gates, x2, mod, final_g, seq, final_norm):
    t, d = x2.shape
    tm = dest_tiles.shape[-1] // TOP_K
    tps = seq // tm
    return pl.pallas_call(
        functools.partial(_combine_kernel, final_norm=final_norm),
        grid=(t // tm,),
        in_specs=[
            pl.BlockSpec((1, 1, TOP_K * tm), lambda i: (i, 0, 0), memory_space=pltpu.SMEM),
            pl.BlockSpec(memory_space=pl.ANY),
            pl.BlockSpec((tm, TOP_K), lambda i: (i, 0)),
            pl.BlockSpec((tm, d), lambda i: (i, 0)),
            pl.BlockSpec((1, 3, d), lambda i: (i // tps, 0, 0)),
            pl.BlockSpec((1, d), lambda i: (0, 0)),
        ],
        out_specs=pl.BlockSpec((tm, d), lambda i: (i, 0)),
        out_shape=jax.ShapeDtypeStruct((t, d), _F32),
        scratch_shapes=[pltpu.VMEM((TOP_K, tm, d), _F32), pltpu.SemaphoreType.DMA(())],
        compiler_params=_cparams(("arbitrary",)),
        name="moe_combine",
    )(dest_tiles, ys, gates, x2, mod, final_g)


def _experts_kernel(te_ref, nv_ref, lim_ref, xs_ref, wg_ref, wu_ref, wd_ref, o_ref, h_ref):
    i = pl.program_id(0)
    k = pl.program_id(1)
    tm = xs_ref.shape[0]

    @pl.when(i < nv_ref[0])
    def _():
        @pl.when(k == 0)
        def _():
            rows = i * tm + lax.broadcasted_iota(jnp.int32, (tm, 1), 0)
            h_ref[...] = jnp.where(rows < lim_ref[i], xs_ref[...], 0.0).astype(_BF16)
            o_ref[...] = jnp.zeros_like(o_ref)

        o_ref[...] += _swiglu_tile(h_ref[...], wg_ref[0], wu_ref[0], wd_ref[0])


def _experts(xs, tile_expert, n_valid, row_limit, w_gate, w_up, w_down, tm):
    r, d = xs.shape
    ne, _, f = w_gate.shape
    tf = _pick(f, (1024, 512, 256, 128))
    nk = f // tf
    n_tiles = r // tm

    def row_map(i, k, te, nv, lim):
        return (jnp.minimum(i, nv[0] - 1), 0)

    def kk(i, k, nv):
        return jnp.where(i < nv[0], k, nk - 1)

    return pl.pallas_call(
        _experts_kernel,
        grid_spec=pltpu.PrefetchScalarGridSpec(
            num_scalar_prefetch=3,
            grid=(n_tiles, nk),
            in_specs=[
                pl.BlockSpec((tm, d), row_map),
                pl.BlockSpec((1, d, tf), lambda i, k, te, nv, lim: (te[i], 0, kk(i, k, nv))),
                pl.BlockSpec((1, d, tf), lambda i, k, te, nv, lim: (te[i], 0, kk(i, k, nv))),
                pl.BlockSpec((1, tf, d), lambda i, k, te, nv, lim: (te[i], kk(i, k, nv), 0)),
            ],
            out_specs=pl.BlockSpec((tm, d), row_map),
            scratch_shapes=[pltpu.VMEM((tm, d), _BF16)],
        ),
        out_shape=jax.ShapeDtypeStruct((r, d), _F32),
        compiler_params=_cparams(("arbitrary", "arbitrary")),
        name="moe_experts",
    )(tile_expert, n_valid, row_limit, xs, w_gate, w_up, w_down)


def _moe_layer(x2, mod, norm_g, w_router, w_gate, w_up, w_down, final_g, seq, final_norm):
    t, d = x2.shape
    ne = w_router.shape[1]
    h, route, cnt = _route(x2, mod, norm_g, w_router.T.astype(_BF16), seq)
    tm = _pick(seq, (512, 256, 128))
    counts = cnt[:, 0].astype(jnp.int32)
    padded = (counts + tm - 1) // tm * tm
    pend = jnp.cumsum(padded)
    pstart = pend - padded
    idx = route[0:TOP_K].astype(jnp.int32)
    rank = route[TOP_K:2 * TOP_K].astype(jnp.int32)
    dest = rank
    for e in range(ne):
        dest = dest + jnp.where(idx == e, pstart[e], 0)
    gates = route[2 * TOP_K:3 * TOP_K].T
    n_tiles = (TOP_K * t) // tm + ne
    dest_tiles = dest.reshape(TOP_K, t // tm, tm).transpose(1, 0, 2).reshape(t // tm, 1, TOP_K * tm)
    tile_start = jnp.arange(n_tiles, dtype=jnp.int32) * tm
    tile_expert = jnp.minimum(jnp.sum(pend[None, :] <= tile_start[:, None], axis=1), ne - 1).astype(jnp.int32)
    n_valid = (pend[-1] // tm).astype(jnp.int32).reshape(1)
    row_limit = jnp.zeros((n_tiles,), jnp.int32)
    for e in range(ne):
        row_limit = row_limit + jnp.where(tile_expert == e, pstart[e] + counts[e], 0)
    xs = _dispatch(h, dest_tiles, n_tiles * tm)
    ys = _experts(xs, tile_expert, n_valid, row_limit, w_gate, w_up, w_down, tm)
    return _combine(ys, dest_tiles, gates, x2, mod, final_g, seq, final_norm)


def _final_norm_kernel(x_ref, g_ref, o_ref):
    x = x_ref[...]
    o_ref[...] = x * lax.rsqrt(jnp.mean(x * x, axis=-1, keepdims=True) + EPS) * g_ref[...]


def _final_norm(x2, fg):
    t, d = x2.shape
    tm = _pick(t, (512, 256, 128))
    return pl.pallas_call(
        _final_norm_kernel,
        grid=(t // tm,),
        in_specs=[pl.BlockSpec((tm, d), lambda i: (i, 0)), pl.BlockSpec((1, d), lambda i: (0, 0))],
        out_specs=pl.BlockSpec((tm, d), lambda i: (i, 0)),
        out_shape=jax.ShapeDtypeStruct((t, d), _F32),
        compiler_params=_cparams(("arbitrary",)),
        name="final_norm",
    )(x2, fg)


def kernel(x, c, w_ada, b_ada, norm_g, w_in, ln_g, ln_b, w_s, b_s, w_pool, pool_scale, w_oa, w_ob, w_out,
           ffn_w1, ffn_w3, ffn_w2, w_router, moe_w_gate, moe_w_up, moe_w_down, final_g):
    bsz, seq, d = x.shape
    depth = w_in.shape[0]
    t = bsz * seq
    x2 = x.reshape(t, d)
    mods = _adaln_all(c, w_ada.reshape(depth * 2, d, 3 * d), b_ada.reshape(depth * 2, 1, 3 * d))
    mods = mods.reshape(depth * 2, bsz, 3, d)
    fg = final_g.reshape(1, d)
    bf = lambda w: w.astype(_BF16)
    moe_bf16 = {}
    for i in range(depth):
        last = i == depth - 1
        a, b, gates = _inproj(x2, mods[2 * i], norm_g[i, 0].reshape(1, d), bf(w_in[i]),
                              ln_g[i].reshape(1, -1), ln_b[i].reshape(1, -1), w_s[i], b_s[i].T,
                              bf(w_pool[i]), pool_scale[i].reshape(1, -1), seq)
        x2 = _outproj(a, b, gates, x2, mods[2 * i], bf(w_oa[i]), bf(w_ob[i]), bf(w_out[i]), seq)
        j = i // 2
        ng = norm_g[i, 1].reshape(1, d)
        if i % 2 == 0:
            nxt = (i + 1) // 2
            pending = (moe_w_gate[nxt], moe_w_up[nxt], moe_w_down[nxt]) if i + 1 < depth else ()
            x2, converted = _dense_ffn(x2, mods[2 * i + 1], ng, bf(ffn_w1[j]), bf(ffn_w3[j]), bf(ffn_w2[j]),
                                       seq, cast_srcs=pending)
            if converted is not None:
                moe_bf16[nxt] = converted
            if last:
                x2 = _final_norm(x2, fg)
        else:
            wg, wu, wd = moe_bf16.get(j) or (bf(moe_w_gate[j]), bf(moe_w_up[j]), bf(moe_w_down[j]))
            x2 = _moe_layer(x2, mods[2 * i + 1], ng, w_router[j], wg, wu, wd, fg, seq, final_norm=last)
    return x2.reshape(bsz, seq, d)
```

```python
import functools

import jax
import jax.numpy as jnp
from jax import lax
from jax.experimental import pallas as pl
from jax.experimental.pallas import tpu as pltpu

EPS = 1e-6
POOL_WINDOWS = (2, 4, 8, 16)
TOP_K = 2
POOL_HALO = 16
V7X_MXU_COLS = 256
V7X_VMEM_LIMIT_BYTES = 56 * 1024 * 1024
BF16_SUBLANE_TILE = 16
DMA_ISSUE_UNROLL = 8

_BF16 = jnp.bfloat16
_F32 = jnp.float32


def _cparams(sem):
    return pltpu.CompilerParams(dimension_semantics=sem, vmem_limit_bytes=V7X_VMEM_LIMIT_BYTES)


def _gelu_tanh(x):
    return 0.5 * x * (1.0 + jnp.tanh(0.7978845608028654 * (x + 0.044715 * (x * x * x))))


def _silu(x):
    return x * jax.nn.sigmoid(x)


def _modnorm(x, g, shift, scale):
    y = x * lax.rsqrt(jnp.mean(x * x, axis=-1, keepdims=True) + EPS)
    return (y * g) * (1.0 + scale) + shift


def _pick(n, pref):
    for t in pref:
        if n % t == 0:
            return t
    return n


def _ada_kernel(c_ref, w_ref, b_ref, o_ref):
    sc = _silu(c_ref[...]).astype(_BF16)
    o_ref[0] = jnp.dot(sc, w_ref[0].astype(_BF16), preferred_element_type=_F32) + b_ref[0]


def _adaln_all(c, w_ada, b_ada):
    ns, d, d3 = w_ada.shape
    b = c.shape[0]
    tn = _pick(d3, (768, 512, 256, 128))
    return pl.pallas_call(
        _ada_kernel,
        grid=(ns, d3 // tn),
        in_specs=[
            pl.BlockSpec((b, d), lambda s, j: (0, 0)),
            pl.BlockSpec((1, d, tn), lambda s, j: (s, 0, j)),
            pl.BlockSpec((1, 1, tn), lambda s, j: (s, 0, j)),
        ],
        out_specs=pl.BlockSpec((1, b, tn), lambda s, j: (s, 0, j)),
        out_shape=jax.ShapeDtypeStruct((ns, b, d3), _F32),
        compiler_params=_cparams(("arbitrary", "arbitrary")),
        name="adaln_mod",
    )(c, w_ada, b_ada)


def _inproj_kernel(x_ref, mod_ref, g_ref, w_ref, lng_ref, lnb_ref, ws_ref, bst_ref, wp_ref, ps_ref,
                   a_ref, b_ref, gates_ref,
                   h_ref, u_ref, z_ref, vn_ref, halo_ref,
                   *, tiles_per_seq, heads, chunk, windows):
    i = pl.program_id(0)
    j = pl.program_id(1)
    tm = x_ref.shape[0]
    tn = w_ref.shape[1]
    sw = min(V7X_MXU_COLS, tn)

    def zslab(lo, width):
        return jnp.dot(h_ref[...], w_ref[:, lo:lo + width], preferred_element_type=_F32)

    def sgu_epilogue():
        rsum = jnp.zeros((tm, 1), _F32)
        for lo in range(0, tn, sw):
            rsum = rsum + jnp.sum(z_ref[:, lo:lo + sw], axis=-1, keepdims=True)
        mu = rsum * (1.0 / tn)
        vsum = jnp.zeros((tm, 1), _F32)
        for lo in range(0, tn, sw):
            vc = z_ref[:, lo:lo + sw] - mu
            vsum = vsum + jnp.sum(vc * vc, axis=-1, keepdims=True)
        rstd = lax.rsqrt(vsum * (1.0 / tn) + EPS)
        for lo in range(0, tn, sw):
            cs = slice(lo, lo + sw)
            vn_ref[:, cs] = ((z_ref[:, cs] - mu) * rstd * lng_ref[:, cs] + lnb_ref[:, cs]).astype(_BF16)
        hd = tn // heads
        row = lax.broadcasted_iota(jnp.int32, (chunk, chunk), 0)
        col = lax.broadcasted_iota(jnp.int32, (chunk, chunk), 1)
        causal = col <= row
        for hh in range(heads):
            wm = jnp.where(causal, ws_ref[hh], 0.0).astype(_BF16)
            bias = bst_ref[:, hh:hh + 1]
            for cc in range(tm // chunk):
                rs = slice(cc * chunk, (cc + 1) * chunk)
                cs = slice(hh * hd, (hh + 1) * hd)
                mixed = jnp.dot(wm, vn_ref[rs, cs], preferred_element_type=_F32) + bias
                a_ref[rs, cs] = (u_ref[rs, cs].astype(_F32) * mixed).astype(a_ref.dtype)

    def pool_epilogue():
        first = (i % tiles_per_seq) == 0
        pos = (i % tiles_per_seq) * tm + lax.broadcasted_iota(jnp.int32, (tm, 1), 0) + 1
        gd = tn // len(windows)
        for gi, win in enumerate(windows):
            cs = slice(gi * gd, (gi + 1) * gd)
            z = z_ref[:, cs]
            hist = jnp.where(first, 0.0, halo_ref[:, cs])
            acc = jnp.concatenate([hist, z], axis=0)
            halo_ref[:, cs] = z[tm - POOL_HALO:, :]
            step = 1
            while step < win:
                acc = acc + pltpu.roll(acc, step, 0)
                step *= 2
            cnt = jnp.minimum(pos, win).astype(_F32)
            pooled = acc[POOL_HALO:, :] / cnt - z
            y = jnp.dot(pooled.astype(_BF16), wp_ref[gi], preferred_element_type=_F32)
            b_ref[:, cs] = (y * ps_ref[:, cs]).astype(b_ref.dtype)

    @pl.when(j == 0)
    def _():
        m = mod_ref[0]
        h_ref[...] = _modnorm(x_ref[...], g_ref[...], m[0:1, :], m[1:2, :]).astype(_BF16)
        for lo in range(0, tn, sw):
            u_ref[:, lo:lo + sw] = _gelu_tanh(zslab(lo, sw)).astype(u_ref.dtype)

    @pl.when(j == 1)
    def _():
        for lo in range(0, tn, sw):
            z_ref[:, lo:lo + sw] = _gelu_tanh(zslab(lo, sw))
        sgu_epilogue()

    @pl.when(j == 2)
    def _():
        for lo in range(0, tn, sw):
            z_ref[:, lo:lo + sw] = zslab(lo, sw)
        pool_epilogue()

    @pl.when(j >= 3)
    def _():
        for lo in range(0, tn, sw):
            gates_ref[:, lo:lo + sw] = jax.nn.sigmoid(zslab(lo, sw)).astype(gates_ref.dtype)


def _inproj(x2, mod, norm_g, w_in, ln_g, ln_b, w_s, b_s_t, w_pool, pool_scale, seq):
    t, d = x2.shape
    d_in = w_in.shape[1]
    tn = ln_g.shape[-1]
    heads, chunk, _ = w_s.shape
    groups, gd, _ = w_pool.shape
    assert pool_scale.shape[-1] == tn and groups * gd == tn and d % tn == 0
    assert d_in == 3 * tn + 2 * d and max(POOL_WINDOWS[:groups]) <= POOL_HALO
    tm = _pick(seq, (1024, 512, 256, 128))
    assert tm % chunk == 0 and tm >= POOL_HALO
    kern = functools.partial(_inproj_kernel, tiles_per_seq=seq // tm, heads=heads, chunk=chunk,
                             windows=POOL_WINDOWS[:groups])
    tps = seq // tm
    const2 = lambda i, j: (0, 0)
    const3 = lambda i, j: (0, 0, 0)
    return pl.pallas_call(
        kern,
        grid=(t // tm, d_in // tn),
        in_specs=[
            pl.BlockSpec((tm, d), lambda i, j: (i, 0)),
            pl.BlockSpec((1, 3, d), lambda i, j: (i // tps, 0, 0)),
            pl.BlockSpec((1, d), const2),
            pl.BlockSpec((d, tn), lambda i, j: (0, j)),
            pl.BlockSpec((1, tn), const2),
            pl.BlockSpec((1, tn), const2),
            pl.BlockSpec((heads, chunk, chunk), const3),
            pl.BlockSpec((chunk, heads), const2),
            pl.BlockSpec((groups, gd, gd), const3),
            pl.BlockSpec((1, tn), const2),
        ],
        out_specs=[
            pl.BlockSpec((tm, tn), lambda i, j: (i, 0)),
            pl.BlockSpec((tm, tn), lambda i, j: (i, 0)),
            pl.BlockSpec((tm, tn), lambda i, j: (i, jnp.maximum(j - 3, 0))),
        ],
        out_shape=[
            jax.ShapeDtypeStruct((t, tn), _BF16),
            jax.ShapeDtypeStruct((t, tn), _BF16),
            jax.ShapeDtypeStruct((t, 2 * d), _BF16),
        ],
        scratch_shapes=[
            pltpu.VMEM((tm, d), _BF16),
            pltpu.VMEM((tm, tn), _BF16),
            pltpu.VMEM((tm, tn), _F32),
            pltpu.VMEM((tm, tn), _BF16),
            pltpu.VMEM((POOL_HALO, tn), _F32),
        ],
        compiler_params=_cparams(("arbitrary", "arbitrary")),
        name="mixer_inproj",
    )(x2, mod, norm_g, w_in, ln_g, ln_b, w_s, b_s_t, w_pool, pool_scale)


def _outproj_kernel(a_ref, b_ref, sga_ref, sgb_ref, x_ref, mod_ref, woa_ref, wob_ref, wout_ref,
                    o_ref, m_ref, *, tc):
    d = x_ref.shape[1]
    for c in range(d // tc):
        cs = slice(c * tc, (c + 1) * tc)
        ya = jnp.dot(a_ref[...], woa_ref[:, cs], preferred_element_type=_F32)
        yb = jnp.dot(b_ref[...], wob_ref[:, cs], preferred_element_type=_F32)
        m_ref[:, cs] = (sga_ref[:, cs].astype(_F32) * ya + sgb_ref[:, cs].astype(_F32) * yb).astype(_BF16)
    gate = mod_ref[0][2:3, :]
    for c in range(d // tc):
        cs = slice(c * tc, (c + 1) * tc)
        y = jnp.dot(m_ref[...], wout_ref[:, cs], preferred_element_type=_F32)
        o_ref[:, cs] = x_ref[:, cs] + gate[:, cs] * y


def _outproj(a, b, gates, x2, mod, w_oa, w_ob, w_out, seq):
    t, d = x2.shape
    ds = a.shape[1]
    tm = _pick(seq, (512, 256, 128))
    tc = _pick(d, (512, 256, 128))
    tps = seq // tm
    once = pl.Buffered(1)
    return pl.pallas_call(
        functools.partial(_outproj_kernel, tc=tc),
        grid=(t // tm,),
        in_specs=[
            pl.BlockSpec((tm, ds), lambda i: (i, 0)),
            pl.BlockSpec((tm, ds), lambda i: (i, 0)),
            pl.BlockSpec((tm, d), lambda i: (i, 0)),
            pl.BlockSpec((tm, d), lambda i: (i, 1)),
            pl.BlockSpec((tm, d), lambda i: (i, 0)),
            pl.BlockSpec((1, 3, d), lambda i: (i // tps, 0, 0)),
            pl.BlockSpec((ds, d), lambda i: (0, 0), pipeline_mode=once),
            pl.BlockSpec((ds, d), lambda i: (0, 0), pipeline_mode=once),
            pl.BlockSpec((d, d), lambda i: (0, 0), pipeline_mode=once),
        ],
        out_specs=pl.BlockSpec((tm, d), lambda i: (i, 0)),
        out_shape=jax.ShapeDtypeStruct((t, d), _F32),
        scratch_shapes=[pltpu.VMEM((tm, d), _BF16)],
        compiler_params=_cparams(("arbitrary",)),
        name="mixer_outproj",
    )(a, b, gates, gates, x2, mod, w_oa, w_ob, w_out)


def _swiglu_tile(h, wg, wu, wd):
    gte = jnp.dot(h, wg, preferred_element_type=_F32)
    up = jnp.dot(h, wu, preferred_element_type=_F32)
    act = (_silu(gte) * up).astype(_BF16)
    return jnp.dot(act, wd, preferred_element_type=_F32)


def _ffn_kernel(*refs, n_cast, cast_k, norm_k):
    x_ref, mod_ref, mcur_ref, g_ref, w1_ref, w3_ref, w2_ref = refs[:7]
    cast_in = refs[7:7 + n_cast]
    o_ref = refs[7 + n_cast]
    cast_out = refs[8 + n_cast:8 + 2 * n_cast]
    h_ref, hn_ref = refs[8 + 2 * n_cast:]
    i = pl.program_id(0)
    k = pl.program_id(1)

    def norm_next():
        m = mod_ref[0]
        hn_ref[...] = _modnorm(x_ref[...], g_ref[...], m[0:1, :], m[1:2, :]).astype(_BF16)

    def gated_tile(lhs_ref):
        return mcur_ref[0][2:3, :] * _swiglu_tile(lhs_ref[...], w1_ref[...], w3_ref[...], w2_ref[...])

    def cast_blocks():
        for src, dst in zip(cast_in, cast_out):
            dst[...] = src[...].astype(dst.dtype)

    @pl.when((i == 0) & (k == 0))
    def _():
        norm_next()

    @pl.when(k == 0)
    def _():
        o_ref[...] = x_ref[...] + gated_tile(hn_ref)
        h_ref[...] = hn_ref[...]
        if cast_k > 0:
            cast_blocks()

    @pl.when((k > 0) & (k < cast_k))
    def _():
        o_ref[...] += gated_tile(h_ref)
        cast_blocks()

    @pl.when(k == norm_k)
    def _():
        o_ref[...] += gated_tile(h_ref)
        norm_next()

    @pl.when((k >= max(cast_k, 1)) & (k != norm_k))
    def _():
        o_ref[...] += gated_tile(h_ref)


def _cast_plan(arrays, n_i, nk):
    for c in range(max(nk - 2, 0), 0, -1):
        if all(a.shape[0] % (n_i * c) == 0 and (a.shape[0] // (n_i * c)) % BF16_SUBLANE_TILE == 0 for a in arrays):
            return c
    return None


def _dense_ffn(x2, mod, norm_g, w1, w3, w2, seq, cast_srcs=()):
    t, d = x2.shape
    f = w1.shape[1]
    tm = _pick(seq, (512, 256, 128))
    tf = _pick(f, (512, 256, 128))
    tps = seq // tm
    n_i, nk = t // tm, f // tf
    srcs2d = [s.reshape(-1, s.shape[-1]) for s in cast_srcs]
    cast_k = _cast_plan(srcs2d, n_i, nk) if srcs2d else None
    if cast_k is None:
        srcs2d, cast_k = [], 0
    norm_k = nk - 1 if cast_k == 0 else max(cast_k, nk - 2)
    assert nk >= 2 and norm_k >= cast_k

    def x_tile(i, k):
        return jnp.minimum(i + (k >= norm_k).astype(jnp.int32), n_i - 1)

    def cast_block(i, k):
        return (i * cast_k + jnp.minimum(k, cast_k - 1), 0)

    cast_specs = [pl.BlockSpec((s.shape[0] // (n_i * cast_k), s.shape[1]), cast_block) for s in srcs2d]
    outs = pl.pallas_call(
        functools.partial(_ffn_kernel, n_cast=len(srcs2d), cast_k=cast_k, norm_k=norm_k),
        grid=(n_i, nk),
        in_specs=[
            pl.BlockSpec((tm, d), lambda i, k: (x_tile(i, k), 0)),
            pl.BlockSpec((1, 3, d), lambda i, k: (x_tile(i, k) // tps, 0, 0)),
            pl.BlockSpec((1, 3, d), lambda i, k: (i // tps, 0, 0)),
            pl.BlockSpec((1, d), lambda i, k: (0, 0)),
            pl.BlockSpec((d, tf), lambda i, k: (0, k)),
            pl.BlockSpec((d, tf), lambda i, k: (0, k)),
            pl.BlockSpec((tf, d), lambda i, k: (k, 0)),
        ] + cast_specs,
        out_specs=[pl.BlockSpec((tm, d), lambda i, k: (i, 0))] + cast_specs,
        out_shape=[jax.ShapeDtypeStruct((t, d), _F32)] + [jax.ShapeDtypeStruct(s.shape, _BF16) for s in srcs2d],
        scratch_shapes=[pltpu.VMEM((tm, d), _BF16), pltpu.VMEM((tm, d), _BF16)],
        compiler_params=_cparams(("arbitrary", "arbitrary")),
        name="dense_swiglu",
    )(x2, mod, mod, norm_g, w1, w3, w2, *srcs2d)
    if not srcs2d:
        return outs[0], None
    return outs[0], [o.reshape(s.shape) for o, s in zip(outs[1:], cast_srcs)]


def _route_kernel(x_ref, mod_ref, g_ref, wrt_ref, h_ref, route_ref, cnt_ref, run_ref):
    i = pl.program_id(0)
    tm = x_ref.shape[0]
    ne = wrt_ref.shape[0]

    @pl.when(i == 0)
    def _():
        run_ref[...] = jnp.zeros_like(run_ref)

    m = mod_ref[0]
    h = _modnorm(x_ref[...], g_ref[...], m[0:1, :], m[1:2, :])
    h_ref[...] = h
    lt = lax.dot_general(wrt_ref[...], h.astype(_BF16), (((1,), (1,)), ((), ())),
                         preferred_element_type=_F32)
    eid = lax.broadcasted_iota(jnp.int32, (ne, tm), 0).astype(_F32)
    m1 = jnp.max(lt, axis=0, keepdims=True)
    i1 = jnp.min(jnp.where(lt == m1, eid, float(ne)), axis=0, keepdims=True)
    l2 = jnp.where(eid == i1, -jnp.inf, lt)
    m2 = jnp.max(l2, axis=0, keepdims=True)
    i2 = jnp.min(jnp.where(l2 == m2, eid, float(ne)), axis=0, keepdims=True)
    e2 = jnp.exp(m2 - m1)
    g1 = 1.0 / (1.0 + e2)
    g2 = e2 / (1.0 + e2)
    oh1 = eid == i1
    oh2 = eid == i2
    both = jnp.where(oh1 | oh2, 1.0, 0.0).astype(_BF16)
    r = lax.broadcasted_iota(jnp.int32, (tm, tm), 0)
    c = lax.broadcasted_iota(jnp.int32, (tm, tm), 1)
    upper = jnp.where(r <= c, 1.0, 0.0).astype(_BF16)
    pref = jnp.dot(both, upper, preferred_element_type=_F32)
    base = run_ref[:, 0:1] + pref - 1.0
    rank1 = jnp.sum(jnp.where(oh1, base, 0.0), axis=0, keepdims=True)
    rank2 = jnp.sum(jnp.where(oh2, base, 0.0), axis=0, keepdims=True)
    run_new = run_ref[:, 0:1] + pref[:, tm - 1:tm]
    run_ref[...] = jnp.broadcast_to(run_new, run_ref.shape)
    cnt_ref[...] = jnp.broadcast_to(run_new, cnt_ref.shape)
    route_ref[0:1, :] = i1
    route_ref[1:2, :] = i2
    route_ref[2:3, :] = rank1
    route_ref[3:4, :] = rank2
    route_ref[4:5, :] = g1
    route_ref[5:6, :] = g2
    route_ref[6:8, :] = jnp.zeros((2, tm), _F32)


def _route(x2, mod, norm_g, w_router_t, seq):
    t, d = x2.shape
    ne = w_router_t.shape[0]
    tm = _pick(seq, (512, 256, 128))
    tps = seq // tm
    return pl.pallas_call(
        _route_kernel,
        grid=(t // tm,),
        in_specs=[
            pl.BlockSpec((tm, d), lambda i: (i, 0)),
            pl.BlockSpec((1, 3, d), lambda i: (i // tps, 0, 0)),
            pl.BlockSpec((1, d), lambda i: (0, 0)),
            pl.BlockSpec((ne, d), lambda i: (0, 0)),
        ],
        out_specs=[
            pl.BlockSpec((tm, d), lambda i: (i, 0)),
            pl.BlockSpec((8, tm), lambda i: (0, i)),
            pl.BlockSpec((ne, 128), lambda i: (0, 0)),
        ],
        out_shape=[
            jax.ShapeDtypeStruct((t, d), _F32),
            jax.ShapeDtypeStruct((8, t), _F32),
            jax.ShapeDtypeStruct((ne, 128), _F32),
        ],
        scratch_shapes=[pltpu.VMEM((ne, 128), _F32)],
        compiler_params=_cparams(("arbitrary",)),
        name="moe_route",
    )(x2, mod, norm_g, w_router_t)


def _row_copy(src_ref, src_row, dst_ref, dst_row, sem):
    return pltpu.make_async_copy(src_ref.at[pl.ds(src_row, 1)], dst_ref.at[pl.ds(dst_row, 1)], sem)


def _dispatch_kernel(dest_ref, h_ref, xs_ref, sem):
    tm = h_ref.shape[0]

    def issue(r, carry):
        for kk in range(TOP_K):
            _row_copy(h_ref, r, xs_ref, dest_ref[0, 0, kk * tm + r], sem).start(priority=kk % 2)
        return carry

    lax.fori_loop(0, tm, issue, 0, unroll=DMA_ISSUE_UNROLL)
    for kk in range(TOP_K):
        pltpu.make_async_copy(h_ref, xs_ref.at[pl.ds(0, tm)], sem).wait()


def _dispatch(h, dest_tiles, n_rows):
    t, d = h.shape
    tm = dest_tiles.shape[-1] // TOP_K
    return pl.pallas_call(
        _dispatch_kernel,
        grid=(t // tm,),
        in_specs=[
            pl.BlockSpec((1, 1, TOP_K * tm), lambda i: (i, 0, 0), memory_space=pltpu.SMEM),
            pl.BlockSpec((tm, d), lambda i: (i, 0)),
        ],
        out_specs=pl.BlockSpec(memory_space=pl.ANY),
        out_shape=jax.ShapeDtypeStruct((n_rows, d), _F32),
        scratch_shapes=[pltpu.SemaphoreType.DMA(())],
        compiler_params=_cparams(("arbitrary",)),
        name="moe_dispatch",
    )(dest_tiles, h)


def _combine_kernel(dest_ref, dest_next_ref, ys_ref, gates_ref, x_ref, mod_ref, fg_ref, o_ref, buf_ref, sems,
                    *, final_norm):
    i = pl.program_id(0)
    tm = x_ref.shape[0]
    slot = i % 2

    def gather(idx_ref, s):
        def issue(r, carry):
            for kk in range(TOP_K):
                _row_copy(ys_ref, idx_ref[0, 0, kk * tm + r], buf_ref.at[s, kk], r, sems.at[s]).start(priority=kk % 2)
            return carry

        lax.fori_loop(0, tm, issue, 0, unroll=DMA_ISSUE_UNROLL)

    @pl.when(i == 0)
    def _():
        gather(dest_ref, 0)

    @pl.when(i + 1 < pl.num_programs(0))
    def _():
        gather(dest_next_ref, 1 - slot)

    for kk in range(TOP_K):
        pltpu.make_async_copy(ys_ref.at[pl.ds(0, tm)], buf_ref.at[slot, kk], sems.at[slot]).wait()
    y = gates_ref[:, 0:1] * buf_ref[slot, 0]
    for kk in range(1, TOP_K):
        y = y + gates_ref[:, kk:kk + 1] * buf_ref[slot, kk]
    xn = x_ref[...] + mod_ref[0][2:3, :] * y
    if final_norm:
        xn = xn * lax.rsqrt(jnp.mean(xn * xn, axis=-1, keepdims=True) + EPS) * fg_ref[...]
    o_ref[...] = xn


def _combine(ys, dest_tiles, gates, x2, mod, final_g, seq, final_norm):
    t, d = x2.shape
    tm = dest_tiles.shape[-1] // TOP_K
    tps = seq // tm
    n_i = t // tm
    return pl.pallas_call(
        functools.partial(_combine_kernel, final_norm=final_norm),
        grid=(n_i,),
        in_specs=[
            pl.BlockSpec((1, 1, TOP_K * tm), lambda i: (i, 0, 0), memory_space=pltpu.SMEM),
            pl.BlockSpec((1, 1, TOP_K * tm), lambda i: (jnp.minimum(i + 1, n_i - 1), 0, 0), memory_space=pltpu.SMEM),
            pl.BlockSpec(memory_space=pl.ANY),
            pl.BlockSpec((tm, TOP_K), lambda i: (i, 0)),
            pl.BlockSpec((tm, d), lambda i: (i, 0)),
            pl.BlockSpec((1, 3, d), lambda i: (i // tps, 0, 0)),
            pl.BlockSpec((1, d), lambda i: (0, 0)),
        ],
        out_specs=pl.BlockSpec((tm, d), lambda i: (i, 0)),
        out_shape=jax.ShapeDtypeStruct((t, d), _F32),
        scratch_shapes=[pltpu.VMEM((2, TOP_K, tm, d), _F32), pltpu.SemaphoreType.DMA((2,))],
        compiler_params=_cparams(("arbitrary",)),
        name="moe_combine",
    )(dest_tiles, dest_tiles, ys, gates, x2, mod, final_g)


def _experts_kernel(te_ref, nv_ref, lim_ref, xs_ref, wg_ref, wu_ref, wd_ref, o_ref, h_ref):
    i = pl.program_id(0)
    k = pl.program_id(1)
    tm = xs_ref.shape[0]

    @pl.when(i < nv_ref[0])
    def _():
        @pl.when(k == 0)
        def _():
            rows = i * tm + lax.broadcasted_iota(jnp.int32, (tm, 1), 0)
            h_ref[...] = jnp.where(rows < lim_ref[i], xs_ref[...], 0.0).astype(_BF16)
            o_ref[...] = jnp.zeros_like(o_ref)

        o_ref[...] += _swiglu_tile(h_ref[...], wg_ref[0], wu_ref[0], wd_ref[0])


def _experts(xs, tile_expert, n_valid, row_limit, w_gate, w_up, w_down, tm):
    r, d = xs.shape
    ne, _, f = w_gate.shape
    tf = _pick(f, (1024, 512, 256, 128))
    nk = f // tf
    n_tiles = r // tm

    def row_map(i, k, te, nv, lim):
        return (jnp.minimum(i, nv[0] - 1), 0)

    def kk(i, k, nv):
        return jnp.where(i < nv[0], k, nk - 1)

    return pl.pallas_call(
        _experts_kernel,
        grid_spec=pltpu.PrefetchScalarGridSpec(
            num_scalar_prefetch=3,
            grid=(n_tiles, nk),
            in_specs=[
                pl.BlockSpec((tm, d), row_map),
                pl.BlockSpec((1, d, tf), lambda i, k, te, nv, lim: (te[i], 0, kk(i, k, nv))),
                pl.BlockSpec((1, d, tf), lambda i, k, te, nv, lim: (te[i], 0, kk(i, k, nv))),
                pl.BlockSpec((1, tf, d), lambda i, k, te, nv, lim: (te[i], kk(i, k, nv), 0)),
            ],
            out_specs=pl.BlockSpec((tm, d), row_map),
            scratch_shapes=[pltpu.VMEM((tm, d), _BF16)],
        ),
        out_shape=jax.ShapeDtypeStruct((r, d), _F32),
        compiler_params=_cparams(("arbitrary", "arbitrary")),
        name="moe_experts",
    )(tile_expert, n_valid, row_limit, xs, w_gate, w_up, w_down)


def _moe_layer(x2, mod, norm_g, w_router, w_gate, w_up, w_down, final_g, seq, final_norm):
    t, d = x2.shape
    ne = w_router.shape[1]
    h, route, cnt = _route(x2, mod, norm_g, w_router.T.astype(_BF16), seq)
    tm = _pick(seq, (512, 256, 128))
    counts = cnt[:, 0].astype(jnp.int32)
    padded = (counts + tm - 1) // tm * tm
    pend = jnp.cumsum(padded)
    pstart = pend - padded
    idx = route[0:TOP_K].astype(jnp.int32)
    rank = route[TOP_K:2 * TOP_K].astype(jnp.int32)
    dest = rank
    for e in range(ne):
        dest = dest + jnp.where(idx == e, pstart[e], 0)
    gates = route[2 * TOP_K:3 * TOP_K].T
    n_tiles = (TOP_K * t) // tm + ne
    dest_tiles = dest.reshape(TOP_K, t // tm, tm).transpose(1, 0, 2).reshape(t // tm, 1, TOP_K * tm)
    tile_start = jnp.arange(n_tiles, dtype=jnp.int32) * tm
    tile_expert = jnp.minimum(jnp.sum(pend[None, :] <= tile_start[:, None], axis=1), ne - 1).astype(jnp.int32)
    n_valid = (pend[-1] // tm).astype(jnp.int32).reshape(1)
    row_limit = jnp.zeros((n_tiles,), jnp.int32)
    for e in range(ne):
        row_limit = row_limit + jnp.where(tile_expert == e, pstart[e] + counts[e], 0)
    xs = _dispatch(h, dest_tiles, n_tiles * tm)
    ys = _experts(xs, tile_expert, n_valid, row_limit, w_gate, w_up, w_down, tm)
    return _combine(ys, dest_tiles, gates, x2, mod, final_g, seq, final_norm)


def _final_norm_kernel(x_ref, g_ref, o_ref):
    x = x_ref[...]
    o_ref[...] = x * lax.rsqrt(jnp.mean(x * x, axis=-1, keepdims=True) + EPS) * g_ref[...]


def _final_norm(x2, fg):
    t, d = x2.shape
    tm = _pick(t, (512, 256, 128))
    return pl.pallas_call(
        _final_norm_kernel,
        grid=(t // tm,),
        in_specs=[pl.BlockSpec((tm, d), lambda i: (i, 0)), pl.BlockSpec((1, d), lambda i: (0, 0))],
        out_specs=pl.BlockSpec((tm, d), lambda i: (i, 0)),
        out_shape=jax.ShapeDtypeStruct((t, d), _F32),
        compiler_params=_cparams(("arbitrary",)),
        name="final_norm",
    )(x2, fg)


def kernel(x, c, w_ada, b_ada, norm_g, w_in, ln_g, ln_b, w_s, b_s, w_pool, pool_scale, w_oa, w_ob, w_out,
           ffn_w1, ffn_w3, ffn_w2, w_router, moe_w_gate, moe_w_up, moe_w_down, final_g):
    bsz, seq, d = x.shape
    depth = w_in.shape[0]
    t = bsz * seq
    x2 = x.reshape(t, d)
    mods = _adaln_all(c, w_ada.reshape(depth * 2, d, 3 * d), b_ada.reshape(depth * 2, 1, 3 * d))
    mods = mods.reshape(depth * 2, bsz, 3, d)
    fg = final_g.reshape(1, d)
    bf = lambda w: w.astype(_BF16)
    moe_bf16 = {}
    for i in range(depth):
        last = i == depth - 1
        a, b, gates = _inproj(x2, mods[2 * i], norm_g[i, 0].reshape(1, d), bf(w_in[i]),
                              ln_g[i].reshape(1, -1), ln_b[i].reshape(1, -1), w_s[i], b_s[i].T,
                              bf(w_pool[i]), pool_scale[i].reshape(1, -1), seq)
        x2 = _outproj(a, b, gates, x2, mods[2 * i], bf(w_oa[i]), bf(w_ob[i]), bf(w_out[i]), seq)
        j = i // 2
        ng = norm_g[i, 1].reshape(1, d)
        if i % 2 == 0:
            nxt = (i + 1) // 2
            pending = (moe_w_gate[nxt], moe_w_up[nxt], moe_w_down[nxt]) if i + 1 < depth else ()
            x2, converted = _dense_ffn(x2, mods[2 * i + 1], ng, bf(ffn_w1[j]), bf(ffn_w3[j]), bf(ffn_w2[j]),
                                       seq, cast_srcs=pending)
            if converted is not None:
                moe_bf16[nxt] = converted
            if last:
                x2 = _final_norm(x2, fg)
        else:
            wg, wu, wd = moe_bf16.get(j) or (bf(moe_w_gate[j]), bf(moe_w_up[j]), bf(moe_w_down[j]))
            x2 = _moe_layer(x2, mods[2 * i + 1], ng, w_router[j], wg, wu, wd, fg, seq, final_norm=last)
    return x2.reshape(bsz, seq, d)
```

```python
import functools

import jax
import jax.numpy as jnp
from jax import lax
from jax.experimental import pallas as pl
from jax.experimental.pallas import tpu as pltpu

EPS = 1e-6
POOL_WINDOWS = (2, 4, 8, 16)
TOP_K = 2
POOL_HALO = 16
V7X_MXU_COLS = 256
V7X_VMEM_LIMIT_BYTES = 56 * 1024 * 1024
BF16_SUBLANE_TILE = 16
DMA_ISSUE_UNROLL = 8

_BF16 = jnp.bfloat16
_F32 = jnp.float32


def _cparams(sem):
    return pltpu.CompilerParams(dimension_semantics=sem, vmem_limit_bytes=V7X_VMEM_LIMIT_BYTES)


def _gelu_tanh(x):
    return 0.5 * x * (1.0 + jnp.tanh(0.7978845608028654 * (x + 0.044715 * (x * x * x))))


def _silu(x):
    return x * jax.nn.sigmoid(x)


def _modnorm(x, g, shift, scale):
    y = x * lax.rsqrt(jnp.mean(x * x, axis=-1, keepdims=True) + EPS)
    return (y * g) * (1.0 + scale) + shift


def _pick(n, pref):
    for t in pref:
        if n % t == 0:
            return t
    return n


def _ada_kernel(c_ref, w_ref, b_ref, o_ref):
    sc = _silu(c_ref[...]).astype(_BF16)
    o_ref[0] = jnp.dot(sc, w_ref[0].astype(_BF16), preferred_element_type=_F32) + b_ref[0]


def _adaln_all(c, w_ada, b_ada):
    ns, d, d3 = w_ada.shape
    b = c.shape[0]
    tn = _pick(d3, (768, 512, 256, 128))
    return pl.pallas_call(
        _ada_kernel,
        grid=(ns, d3 // tn),
        in_specs=[
            pl.BlockSpec((b, d), lambda s, j: (0, 0)),
            pl.BlockSpec((1, d, tn), lambda s, j: (s, 0, j)),
            pl.BlockSpec((1, 1, tn), lambda s, j: (s, 0, j)),
        ],
        out_specs=pl.BlockSpec((1, b, tn), lambda s, j: (s, 0, j)),
        out_shape=jax.ShapeDtypeStruct((ns, b, d3), _F32),
        compiler_params=_cparams(("arbitrary", "arbitrary")),
        name="adaln_mod",
    )(c, w_ada, b_ada)


def _inproj_kernel(x_ref, mod_ref, g_ref, w_ref, lng_ref, lnb_ref, ws_ref, bst_ref, wp_ref, ps_ref,
                   a_ref, b_ref, gates_ref,
                   h_ref, u_ref, z_ref, vn_ref, halo_ref,
                   *, tiles_per_seq, heads, chunk, windows):
    i = pl.program_id(0)
    j = pl.program_id(1)
    tm = x_ref.shape[0]
    tn = w_ref.shape[1]
    sw = min(V7X_MXU_COLS, tn)

    def zslab(lo, width):
        return jnp.dot(h_ref[...], w_ref[:, lo:lo + width], preferred_element_type=_F32)

    def sgu_epilogue():
        rsum = jnp.zeros((tm, 1), _F32)
        for lo in range(0, tn, sw):
            rsum = rsum + jnp.sum(z_ref[:, lo:lo + sw], axis=-1, keepdims=True)
        mu = rsum * (1.0 / tn)
        vsum = jnp.zeros((tm, 1), _F32)
        for lo in range(0, tn, sw):
            vc = z_ref[:, lo:lo + sw] - mu
            vsum = vsum + jnp.sum(vc * vc, axis=-1, keepdims=True)
        rstd = lax.rsqrt(vsum * (1.0 / tn) + EPS)
        for lo in range(0, tn, sw):
            cs = slice(lo, lo + sw)
            vn_ref[:, cs] = ((z_ref[:, cs] - mu) * rstd * lng_ref[:, cs] + lnb_ref[:, cs]).astype(_BF16)
        hd = tn // heads
        row = lax.broadcasted_iota(jnp.int32, (chunk, chunk), 0)
        col = lax.broadcasted_iota(jnp.int32, (chunk, chunk), 1)
        causal = col <= row
        for hh in range(heads):
            wm = jnp.where(causal, ws_ref[hh], 0.0).astype(_BF16)
            bias = bst_ref[:, hh:hh + 1]
            for cc in range(tm // chunk):
                rs = slice(cc * chunk, (cc + 1) * chunk)
                cs = slice(hh * hd, (hh + 1) * hd)
                mixed = jnp.dot(wm, vn_ref[rs, cs], preferred_element_type=_F32) + bias
                a_ref[rs, cs] = (u_ref[rs, cs].astype(_F32) * mixed).astype(a_ref.dtype)

    def pool_epilogue():
        first = (i % tiles_per_seq) == 0
        pos = (i % tiles_per_seq) * tm + lax.broadcasted_iota(jnp.int32, (tm, 1), 0) + 1
        gd = tn // len(windows)
        for gi, win in enumerate(windows):
            cs = slice(gi * gd, (gi + 1) * gd)
            z = z_ref[:, cs]
            hist = jnp.where(first, 0.0, halo_ref[:, cs])
            acc = jnp.concatenate([hist, z], axis=0)
            halo_ref[:, cs] = z[tm - POOL_HALO:, :]
            step = 1
            while step < win:
                acc = acc + pltpu.roll(acc, step, 0)
                step *= 2
            cnt = jnp.minimum(pos, win).astype(_F32)
            pooled = acc[POOL_HALO:, :] / cnt - z
            y = jnp.dot(pooled.astype(_BF16), wp_ref[gi], preferred_element_type=_F32)
            b_ref[:, cs] = (y * ps_ref[:, cs]).astype(b_ref.dtype)

    @pl.when(j == 0)
    def _():
        m = mod_ref[0]
        h_ref[...] = _modnorm(x_ref[...], g_ref[...], m[0:1, :], m[1:2, :]).astype(_BF16)
        for lo in range(0, tn, sw):
            u_ref[:, lo:lo + sw] = _gelu_tanh(zslab(lo, sw)).astype(u_ref.dtype)

    @pl.when(j == 1)
    def _():
        for lo in range(0, tn, sw):
            z_ref[:, lo:lo + sw] = _gelu_tanh(zslab(lo, sw))
        sgu_epilogue()

    @pl.when(j == 2)
    def _():
        for lo in range(0, tn, sw):
            z_ref[:, lo:lo + sw] = zslab(lo, sw)
        pool_epilogue()

    @pl.when(j >= 3)
    def _():
        for lo in range(0, tn, sw):
            gates_ref[:, lo:lo + sw] = jax.nn.sigmoid(zslab(lo, sw)).astype(gates_ref.dtype)


def _inproj(x2, mod, norm_g, w_in, ln_g, ln_b, w_s, b_s_t, w_pool, pool_scale, seq):
    t, d = x2.shape
    d_in = w_in.shape[1]
    tn = ln_g.shape[-1]
    heads, chunk, _ = w_s.shape
    groups, gd, _ = w_pool.shape
    assert pool_scale.shape[-1] == tn and groups * gd == tn and d % tn == 0
    assert d_in == 3 * tn + 2 * d and max(POOL_WINDOWS[:groups]) <= POOL_HALO
    tm = _pick(seq, (1024, 512, 256, 128))
    assert tm % chunk == 0 and tm >= POOL_HALO
    kern = functools.partial(_inproj_kernel, tiles_per_seq=seq // tm, heads=heads, chunk=chunk,
                             windows=POOL_WINDOWS[:groups])
    tps = seq // tm
    const2 = lambda i, j: (0, 0)
    const3 = lambda i, j: (0, 0, 0)
    return pl.pallas_call(
        kern,
        grid=(t // tm, d_in // tn),
        in_specs=[
            pl.BlockSpec((tm, d), lambda i, j: (i, 0)),
            pl.BlockSpec((1, 3, d), lambda i, j: (i // tps, 0, 0)),
            pl.BlockSpec((1, d), const2),
            pl.BlockSpec((d, tn), lambda i, j: (0, j)),
            pl.BlockSpec((1, tn), const2),
            pl.BlockSpec((1, tn), const2),
            pl.BlockSpec((heads, chunk, chunk), const3),
            pl.BlockSpec((chunk, heads), const2),
            pl.BlockSpec((groups, gd, gd), const3),
            pl.BlockSpec((1, tn), const2),
        ],
        out_specs=[
            pl.BlockSpec((tm, tn), lambda i, j: (i, 0)),
            pl.BlockSpec((tm, tn), lambda i, j: (i, 0)),
            pl.BlockSpec((tm, tn), lambda i, j: (i, jnp.maximum(j - 3, 0))),
        ],
        out_shape=[
            jax.ShapeDtypeStruct((t, tn), _BF16),
            jax.ShapeDtypeStruct((t, tn), _BF16),
            jax.ShapeDtypeStruct((t, 2 * d), _BF16),
        ],
        scratch_shapes=[
            pltpu.VMEM((tm, d), _BF16),
            pltpu.VMEM((tm, tn), _BF16),
            pltpu.VMEM((tm, tn), _F32),
            pltpu.VMEM((tm, tn), _BF16),
            pltpu.VMEM((POOL_HALO, tn), _F32),
        ],
        compiler_params=_cparams(("arbitrary", "arbitrary")),
        name="mixer_inproj",
    )(x2, mod, norm_g, w_in, ln_g, ln_b, w_s, b_s_t, w_pool, pool_scale)


def _outproj_kernel(a_ref, b_ref, sga_ref, sgb_ref, x_ref, mod_ref, woa_ref, wob_ref, wout_ref,
                    o_ref, m_ref, *, tc):
    d = x_ref.shape[1]
    for c in range(d // tc):
        cs = slice(c * tc, (c + 1) * tc)
        ya = jnp.dot(a_ref[...], woa_ref[:, cs], preferred_element_type=_F32)
        yb = jnp.dot(b_ref[...], wob_ref[:, cs], preferred_element_type=_F32)
        m_ref[:, cs] = (sga_ref[:, cs].astype(_F32) * ya + sgb_ref[:, cs].astype(_F32) * yb).astype(_BF16)
    gate = mod_ref[0][2:3, :]
    for c in range(d // tc):
        cs = slice(c * tc, (c + 1) * tc)
        y = jnp.dot(m_ref[...], wout_ref[:, cs], preferred_element_type=_F32)
        o_ref[:, cs] = x_ref[:, cs] + gate[:, cs] * y


def _outproj(a, b, gates, x2, mod, w_oa, w_ob, w_out, seq):
    t, d = x2.shape
    ds = a.shape[1]
    tm = _pick(seq, (512, 256, 128))
    tc = _pick(d, (512, 256, 128))
    tps = seq // tm
    once = pl.Buffered(1)
    return pl.pallas_call(
        functools.partial(_outproj_kernel, tc=tc),
        grid=(t // tm,),
        in_specs=[
            pl.BlockSpec((tm, ds), lambda i: (i, 0)),
            pl.BlockSpec((tm, ds), lambda i: (i, 0)),
            pl.BlockSpec((tm, d), lambda i: (i, 0)),
            pl.BlockSpec((tm, d), lambda i: (i, 1)),
            pl.BlockSpec((tm, d), lambda i: (i, 0)),
            pl.BlockSpec((1, 3, d), lambda i: (i // tps, 0, 0)),
            pl.BlockSpec((ds, d), lambda i: (0, 0), pipeline_mode=once),
            pl.BlockSpec((ds, d), lambda i: (0, 0), pipeline_mode=once),
            pl.BlockSpec((d, d), lambda i: (0, 0), pipeline_mode=once),
        ],
        out_specs=pl.BlockSpec((tm, d), lambda i: (i, 0)),
        out_shape=jax.ShapeDtypeStruct((t, d), _F32),
        scratch_shapes=[pltpu.VMEM((tm, d), _BF16)],
        compiler_params=_cparams(("arbitrary",)),
        name="mixer_outproj",
    )(a, b, gates, gates, x2, mod, w_oa, w_ob, w_out)


def _swiglu_tile(h, wg, wu, wd):
    gte = jnp.dot(h, wg, preferred_element_type=_F32)
    up = jnp.dot(h, wu, preferred_element_type=_F32)
    act = (_silu(gte) * up).astype(_BF16)
    return jnp.dot(act, wd, preferred_element_type=_F32)


def _ffn_kernel(*refs, n_cast, cast_k, norm_k):
    x_ref, mod_ref, mcur_ref, g_ref, w1_ref, w3_ref, w2_ref = refs[:7]
    cast_in = refs[7:7 + n_cast]
    o_ref = refs[7 + n_cast]
    cast_out = refs[8 + n_cast:8 + 2 * n_cast]
    h_ref, hn_ref = refs[8 + 2 * n_cast:]
    i = pl.program_id(0)
    k = pl.program_id(1)

    def norm_next():
        m = mod_ref[0]
        hn_ref[...] = _modnorm(x_ref[...], g_ref[...], m[0:1, :], m[1:2, :]).astype(_BF16)

    def gated_tile(lhs_ref):
        return mcur_ref[0][2:3, :] * _swiglu_tile(lhs_ref[...], w1_ref[...], w3_ref[...], w2_ref[...])

    def cast_blocks():
        for src, dst in zip(cast_in, cast_out):
            dst[...] = src[...].astype(dst.dtype)

    @pl.when((i == 0) & (k == 0))
    def _():
        norm_next()

    @pl.when(k == 0)
    def _():
        o_ref[...] = x_ref[...] + gated_tile(hn_ref)
        h_ref[...] = hn_ref[...]
        if cast_k > 0:
            cast_blocks()

    @pl.when((k > 0) & (k < cast_k))
    def _():
        o_ref[...] += gated_tile(h_ref)
        cast_blocks()

    @pl.when(k == norm_k)
    def _():
        o_ref[...] += gated_tile(h_ref)
        norm_next()

    @pl.when((k >= max(cast_k, 1)) & (k != norm_k))
    def _():
        o_ref[...] += gated_tile(h_ref)


def _cast_plan(arrays, n_i, nk):
    for c in range(max(nk - 2, 0), 0, -1):
        if all(a.shape[0] % (n_i * c) == 0 and (a.shape[0] // (n_i * c)) % BF16_SUBLANE_TILE == 0 for a in arrays):
            return c
    return None


def _dense_ffn(x2, mod, norm_g, w1, w3, w2, seq, cast_srcs=()):
    t, d = x2.shape
    f = w1.shape[1]
    tm = _pick(seq, (512, 256, 128))
    tf = _pick(f, (512, 256, 128))
    tps = seq // tm
    n_i, nk = t // tm, f // tf
    srcs2d = [s.reshape(-1, s.shape[-1]) for s in cast_srcs]
    cast_k = _cast_plan(srcs2d, n_i, nk) if srcs2d else None
    if cast_k is None:
        srcs2d, cast_k = [], 0
    norm_k = nk - 1 if cast_k == 0 else max(cast_k, nk - 2)
    assert nk >= 2 and norm_k >= cast_k

    def x_tile(i, k):
        return jnp.minimum(i + (k >= norm_k).astype(jnp.int32), n_i - 1)

    def cast_block(i, k):
        return (i * cast_k + jnp.minimum(k, cast_k - 1), 0)

    cast_specs = [pl.BlockSpec((s.shape[0] // (n_i * cast_k), s.shape[1]), cast_block) for s in srcs2d]
    outs = pl.pallas_call(
        functools.partial(_ffn_kernel, n_cast=len(srcs2d), cast_k=cast_k, norm_k=norm_k),
        grid=(n_i, nk),
        in_specs=[
            pl.BlockSpec((tm, d), lambda i, k: (x_tile(i, k), 0)),
            pl.BlockSpec((1, 3, d), lambda i, k: (x_tile(i, k) // tps, 0, 0)),
            pl.BlockSpec((1, 3, d), lambda i, k: (i // tps, 0, 0)),
            pl.BlockSpec((1, d), lambda i, k: (0, 0)),
            pl.BlockSpec((d, tf), lambda i, k: (0, k)),
            pl.BlockSpec((d, tf), lambda i, k: (0, k)),
            pl.BlockSpec((tf, d), lambda i, k: (k, 0)),
        ] + cast_specs,
        out_specs=[pl.BlockSpec((tm, d), lambda i, k: (i, 0))] + cast_specs,
        out_shape=[jax.ShapeDtypeStruct((t, d), _F32)] + [jax.ShapeDtypeStruct(s.shape, _BF16) for s in srcs2d],
        scratch_shapes=[pltpu.VMEM((tm, d), _BF16), pltpu.VMEM((tm, d), _BF16)],
        compiler_params=_cparams(("arbitrary", "arbitrary")),
        name="dense_swiglu",
    )(x2, mod, mod, norm_g, w1, w3, w2, *srcs2d)
    if not srcs2d:
        return outs[0], None
    return outs[0], [o.reshape(s.shape) for o, s in zip(outs[1:], cast_srcs)]


def _route_kernel(x_ref, mod_ref, g_ref, wrt_ref, h_ref, route_ref, cnt_ref, run_ref):
    i = pl.program_id(0)
    tm = x_ref.shape[0]
    ne = wrt_ref.shape[0]

    @pl.when(i == 0)
    def _():
        run_ref[...] = jnp.zeros_like(run_ref)

    m = mod_ref[0]
    h = _modnorm(x_ref[...], g_ref[...], m[0:1, :], m[1:2, :])
    h_ref[...] = h
    lt = lax.dot_general(wrt_ref[...], h.astype(_BF16), (((1,), (1,)), ((), ())),
                         preferred_element_type=_F32)
    eid = lax.broadcasted_iota(jnp.int32, (ne, tm), 0).astype(_F32)
    m1 = jnp.max(lt, axis=0, keepdims=True)
    i1 = jnp.min(jnp.where(lt == m1, eid, float(ne)), axis=0, keepdims=True)
    l2 = jnp.where(eid == i1, -jnp.inf, lt)
    m2 = jnp.max(l2, axis=0, keepdims=True)
    i2 = jnp.min(jnp.where(l2 == m2, eid, float(ne)), axis=0, keepdims=True)
    e2 = jnp.exp(m2 - m1)
    g1 = 1.0 / (1.0 + e2)
    g2 = e2 / (1.0 + e2)
    oh1 = eid == i1
    oh2 = eid == i2
    both = jnp.where(oh1 | oh2, 1.0, 0.0).astype(_BF16)
    r = lax.broadcasted_iota(jnp.int32, (tm, tm), 0)
    c = lax.broadcasted_iota(jnp.int32, (tm, tm), 1)
    upper = jnp.where(r <= c, 1.0, 0.0).astype(_BF16)
    pref = jnp.dot(both, upper, preferred_element_type=_F32)
    base = run_ref[:, 0:1] + pref - 1.0
    rank1 = jnp.sum(jnp.where(oh1, base, 0.0), axis=0, keepdims=True)
    rank2 = jnp.sum(jnp.where(oh2, base, 0.0), axis=0, keepdims=True)
    run_new = run_ref[:, 0:1] + pref[:, tm - 1:tm]
    run_ref[...] = jnp.broadcast_to(run_new, run_ref.shape)
    cnt_ref[...] = jnp.broadcast_to(run_new, cnt_ref.shape)
    route_ref[0:1, :] = i1
    route_ref[1:2, :] = i2
    route_ref[2:3, :] = rank1
    route_ref[3:4, :] = rank2
    route_ref[4:5, :] = g1
    route_ref[5:6, :] = g2
    route_ref[6:8, :] = jnp.zeros((2, tm), _F32)


def _route(x2, mod, norm_g, w_router_t, seq):
    t, d = x2.shape
    ne = w_router_t.shape[0]
    tm = _pick(seq, (512, 256, 128))
    tps = seq // tm
    return pl.pallas_call(
        _route_kernel,
        grid=(t // tm,),
        in_specs=[
            pl.BlockSpec((tm, d), lambda i: (i, 0)),
            pl.BlockSpec((1, 3, d), lambda i: (i // tps, 0, 0)),
            pl.BlockSpec((1, d), lambda i: (0, 0)),
            pl.BlockSpec((ne, d), lambda i: (0, 0)),
        ],
        out_specs=[
            pl.BlockSpec((tm, d), lambda i: (i, 0)),
            pl.BlockSpec((8, tm), lambda i: (0, i)),
            pl.BlockSpec((ne, 128), lambda i: (0, 0)),
        ],
        out_shape=[
            jax.ShapeDtypeStruct((t, d), _F32),
            jax.ShapeDtypeStruct((8, t), _F32),
            jax.ShapeDtypeStruct((ne, 128), _F32),
        ],
        scratch_shapes=[pltpu.VMEM((ne, 128), _F32)],
        compiler_params=_cparams(("arbitrary",)),
        name="moe_route",
    )(x2, mod, norm_g, w_router_t)


def _row_copy(src_ref, src_row, dst_ref, dst_row, sem):
    return pltpu.make_async_copy(src_ref.at[pl.ds(src_row, 1)], dst_ref.at[pl.ds(dst_row, 1)], sem)


def _dispatch_kernel(dest_ref, hole_ref, h_ref, xs_ref, zero_ref, sem):
    tm = h_ref.shape[0]
    n_holes = hole_ref.shape[-1]
    zero_ref[...] = jnp.zeros_like(zero_ref)

    def issue(r, carry):
        for kk in range(TOP_K):
            _row_copy(h_ref, r, xs_ref, dest_ref[0, 0, kk * tm + r], sem).start(priority=kk % 2)
        return carry

    lax.fori_loop(0, tm, issue, 0, unroll=DMA_ISSUE_UNROLL)

    def fill(r, carry):
        _row_copy(zero_ref, 0, xs_ref, hole_ref[0, 0, r], sem).start()
        return carry

    lax.fori_loop(0, n_holes, fill, 0, unroll=DMA_ISSUE_UNROLL)
    for kk in range(TOP_K):
        pltpu.make_async_copy(h_ref, xs_ref.at[pl.ds(0, tm)], sem).wait()
    pltpu.make_async_copy(h_ref.at[pl.ds(0, n_holes)], xs_ref.at[pl.ds(0, n_holes)], sem).wait()


def _dispatch(h, dest_tiles, hole_tiles, n_rows):
    t, d = h.shape
    tm = dest_tiles.shape[-1] // TOP_K
    n_holes = hole_tiles.shape[-1]
    assert n_holes <= tm
    return pl.pallas_call(
        _dispatch_kernel,
        grid=(t // tm,),
        in_specs=[
            pl.BlockSpec((1, 1, TOP_K * tm), lambda i: (i, 0, 0), memory_space=pltpu.SMEM),
            pl.BlockSpec((1, 1, n_holes), lambda i: (i, 0, 0), memory_space=pltpu.SMEM),
            pl.BlockSpec((tm, d), lambda i: (i, 0)),
        ],
        out_specs=pl.BlockSpec(memory_space=pl.ANY),
        out_shape=jax.ShapeDtypeStruct((n_rows, d), _F32),
        scratch_shapes=[pltpu.VMEM((8, d), _F32), pltpu.SemaphoreType.DMA(())],
        compiler_params=_cparams(("arbitrary",)),
        name="moe_dispatch",
    )(dest_tiles, hole_tiles, h)


def _combine_kernel(dest_ref, dest_next_ref, ys_ref, gates_ref, x_ref, mod_ref, fg_ref, o_ref, buf_ref, sems,
                    *, final_norm):
    i = pl.program_id(0)
    tm = x_ref.shape[0]
    slot = i % 2

    def gather(idx_ref, s):
        def issue(r, carry):
            for kk in range(TOP_K):
                _row_copy(ys_ref, idx_ref[0, 0, kk * tm + r], buf_ref.at[s, kk], r, sems.at[s]).start(priority=kk % 2)
            return carry

        lax.fori_loop(0, tm, issue, 0, unroll=DMA_ISSUE_UNROLL)

    @pl.when(i == 0)
    def _():
        gather(dest_ref, 0)

    @pl.when(i + 1 < pl.num_programs(0))
    def _():
        gather(dest_next_ref, 1 - slot)

    for kk in range(TOP_K):
        pltpu.make_async_copy(ys_ref.at[pl.ds(0, tm)], buf_ref.at[slot, kk], sems.at[slot]).wait()
    y = gates_ref[:, 0:1] * buf_ref[slot, 0]
    for kk in range(1, TOP_K):
        y = y + gates_ref[:, kk:kk + 1] * buf_ref[slot, kk]
    xn = x_ref[...] + mod_ref[0][2:3, :] * y
    if final_norm:
        xn = xn * lax.rsqrt(jnp.mean(xn * xn, axis=-1, keepdims=True) + EPS) * fg_ref[...]
    o_ref[...] = xn


def _combine(ys, dest_tiles, gates, x2, mod, final_g, seq, final_norm):
    t, d = x2.shape
    tm = dest_tiles.shape[-1] // TOP_K
    tps = seq // tm
    n_i = t // tm
    return pl.pallas_call(
        functools.partial(_combine_kernel, final_norm=final_norm),
        grid=(n_i,),
        in_specs=[
            pl.BlockSpec((1, 1, TOP_K * tm), lambda i: (i, 0, 0), memory_space=pltpu.SMEM),
            pl.BlockSpec((1, 1, TOP_K * tm), lambda i: (jnp.minimum(i + 1, n_i - 1), 0, 0), memory_space=pltpu.SMEM),
            pl.BlockSpec(memory_space=pl.ANY),
            pl.BlockSpec((tm, TOP_K), lambda i: (i, 0)),
            pl.BlockSpec((tm, d), lambda i: (i, 0)),
            pl.BlockSpec((1, 3, d), lambda i: (i // tps, 0, 0)),
            pl.BlockSpec((1, d), lambda i: (0, 0)),
        ],
        out_specs=pl.BlockSpec((tm, d), lambda i: (i, 0)),
        out_shape=jax.ShapeDtypeStruct((t, d), _F32),
        scratch_shapes=[pltpu.VMEM((2, TOP_K, tm, d), _F32), pltpu.SemaphoreType.DMA((2,))],
        compiler_params=_cparams(("arbitrary",)),
        name="moe_combine",
    )(dest_tiles, dest_tiles, ys, gates, x2, mod, final_g)


def _experts_kernel(te_ref, nv_ref, xs_ref, wg_ref, wu_ref, wd_ref, o_ref, h_ref):
    i = pl.program_id(0)
    k = pl.program_id(1)

    @pl.when(k == 0)
    def _():
        o_ref[...] = jnp.zeros_like(o_ref)

    @pl.when(i < nv_ref[0])
    def _():
        @pl.when(k == 0)
        def _():
            h_ref[...] = xs_ref[...].astype(_BF16)

        o_ref[...] += _swiglu_tile(h_ref[...], wg_ref[0], wu_ref[0], wd_ref[0])


def _experts(xs, tile_expert, n_valid, w_gate, w_up, w_down, tm):
    r, d = xs.shape
    ne, _, f = w_gate.shape
    tf = _pick(f, (1024, 512, 256, 128))
    nk = f // tf
    n_tiles = r // tm

    def kk(i, k, nv):
        return jnp.where(i < nv[0], k, nk - 1)

    return pl.pallas_call(
        _experts_kernel,
        grid_spec=pltpu.PrefetchScalarGridSpec(
            num_scalar_prefetch=2,
            grid=(n_tiles, nk),
            in_specs=[
                pl.BlockSpec((tm, d), lambda i, k, te, nv: (jnp.minimum(i, nv[0] - 1), 0)),
                pl.BlockSpec((1, d, tf), lambda i, k, te, nv: (te[i], 0, kk(i, k, nv))),
                pl.BlockSpec((1, d, tf), lambda i, k, te, nv: (te[i], 0, kk(i, k, nv))),
                pl.BlockSpec((1, tf, d), lambda i, k, te, nv: (te[i], kk(i, k, nv), 0)),
            ],
            out_specs=pl.BlockSpec((tm, d), lambda i, k, te, nv: (i, 0)),
            scratch_shapes=[pltpu.VMEM((tm, d), _BF16)],
        ),
        out_shape=jax.ShapeDtypeStruct((r, d), _F32),
        compiler_params=_cparams(("arbitrary", "arbitrary")),
        name="moe_experts",
    )(tile_expert, n_valid, xs, w_gate, w_up, w_down)


def _moe_layer(x2, mod, norm_g, w_router, w_gate, w_up, w_down, final_g, seq, final_norm):
    t, d = x2.shape
    ne = w_router.shape[1]
    h, route, cnt = _route(x2, mod, norm_g, w_router.T.astype(_BF16), seq)
    tm = _pick(seq, (512, 256, 128))
    counts = cnt[:, 0].astype(jnp.int32)
    padded = (counts + tm - 1) // tm * tm
    pend = jnp.cumsum(padded)
    pstart = pend - padded
    idx = route[0:TOP_K].astype(jnp.int32)
    rank = route[TOP_K:2 * TOP_K].astype(jnp.int32)
    dest = rank
    for e in range(ne):
        dest = dest + jnp.where(idx == e, pstart[e], 0)
    gates = route[2 * TOP_K:3 * TOP_K].T
    n_tiles = (TOP_K * t) // tm + ne
    dest_tiles = dest.reshape(TOP_K, t // tm, tm).transpose(1, 0, 2).reshape(t // tm, 1, TOP_K * tm)
    tile_start = jnp.arange(n_tiles, dtype=jnp.int32) * tm
    tile_expert = jnp.minimum(jnp.sum(pend[None, :] <= tile_start[:, None], axis=1), ne - 1).astype(jnp.int32)
    n_valid = (pend[-1] // tm).astype(jnp.int32).reshape(1)
    n_steps = t // tm
    holes_per_step = -(-(ne * tm) // n_steps)
    hid = jnp.minimum(jnp.arange(n_steps * holes_per_step, dtype=jnp.int32), ne * tm - 1)
    pad = padded - counts
    pad_end = jnp.cumsum(pad)
    hole = pend[-1] + (hid - pad_end[-1])
    for e in range(ne - 1, -1, -1):
        hole = jnp.where(hid < pad_end[e], pstart[e] + counts[e] + (hid - (pad_end[e] - pad[e])), hole)
    hole_tiles = hole.reshape(n_steps, 1, holes_per_step)
    xs = _dispatch(h, dest_tiles, hole_tiles, n_tiles * tm)
    ys = _experts(xs, tile_expert, n_valid, w_gate, w_up, w_down, tm)
    return _combine(ys, dest_tiles, gates, x2, mod, final_g, seq, final_norm)


def _final_norm_kernel(x_ref, g_ref, o_ref):
    x = x_ref[...]
    o_ref[...] = x * lax.rsqrt(jnp.mean(x * x, axis=-1, keepdims=True) + EPS) * g_ref[...]


def _final_norm(x2, fg):
    t, d = x2.shape
    tm = _pick(t, (512, 256, 128))
    return pl.pallas_call(
        _final_norm_kernel,
        grid=(t // tm,),
        in_specs=[pl.BlockSpec((tm, d), lambda i: (i, 0)), pl.BlockSpec((1, d), lambda i: (0, 0))],
        out_specs=pl.BlockSpec((tm, d), lambda i: (i, 0)),
        out_shape=jax.ShapeDtypeStruct((t, d), _F32),
        compiler_params=_cparams(("arbitrary",)),
        name="final_norm",
    )(x2, fg)


def kernel(x, c, w_ada, b_ada, norm_g, w_in, ln_g, ln_b, w_s, b_s, w_pool, pool_scale, w_oa, w_ob, w_out,
           ffn_w1, ffn_w3, ffn_w2, w_router, moe_w_gate, moe_w_up, moe_w_down, final_g):
    bsz, seq, d = x.shape
    depth = w_in.shape[0]
    t = bsz * seq
    x2 = x.reshape(t, d)
    mods = _adaln_all(c, w_ada.reshape(depth * 2, d, 3 * d), b_ada.reshape(depth * 2, 1, 3 * d))
    mods = mods.reshape(depth * 2, bsz, 3, d)
    fg = final_g.reshape(1, d)
    bf = lambda w: w.astype(_BF16)
    moe_bf16 = {}
    for i in range(depth):
        last = i == depth - 1
        a, b, gates = _inproj(x2, mods[2 * i], norm_g[i, 0].reshape(1, d), bf(w_in[i]),
                              ln_g[i].reshape(1, -1), ln_b[i].reshape(1, -1), w_s[i], b_s[i].T,
                              bf(w_pool[i]), pool_scale[i].reshape(1, -1), seq)
        x2 = _outproj(a, b, gates, x2, mods[2 * i], bf(w_oa[i]), bf(w_ob[i]), bf(w_out[i]), seq)
        j = i // 2
        ng = norm_g[i, 1].reshape(1, d)
        if i % 2 == 0:
            nxt = (i + 1) // 2
            pending = (moe_w_gate[nxt], moe_w_up[nxt], moe_w_down[nxt]) if i + 1 < depth else ()
            x2, converted = _dense_ffn(x2, mods[2 * i + 1], ng, bf(ffn_w1[j]), bf(ffn_w3[j]), bf(ffn_w2[j]),
                                       seq, cast_srcs=pending)
            if converted is not None:
                moe_bf16[nxt] = converted
            if last:
                x2 = _final_norm(x2, fg)
        else:
            wg, wu, wd = moe_bf16.get(j) or (bf(moe_w_gate[j]), bf(moe_w_up[j]), bf(moe_w_down[j]))
            x2 = _moe_layer(x2, mods[2 * i + 1], ng, w_router[j], wg, wu, wd, fg, seq, final_norm=last)
    return x2.reshape(bsz, seq, d)
```

```python
import functools

import jax
import jax.numpy as jnp
from jax import lax
from jax.experimental import pallas as pl
from jax.experimental.pallas import tpu as pltpu

EPS = 1e-6
POOL_WINDOWS = (2, 4, 8, 16)
TOP_K = 2
POOL_HALO = 16
V7X_MXU_COLS = 256
V7X_VMEM_LIMIT_BYTES = 56 * 1024 * 1024
V7X_LANES = 128
BF16_SUBLANE_TILE = 16
DMA_ISSUE_UNROLL = 8
ROW_TILES_INPROJ = (1024, 512, 256, 128)
ROW_TILES = (512, 256, 128)
HIDDEN_TILES_DENSE = (512, 256, 128)
HIDDEN_TILES_EXPERT = (1024, 512, 256, 128)
OUT_COL_TILES = (512, 256, 128)
ADA_COL_TILES = (768, 512, 256, 128)

_BF16 = jnp.bfloat16
_F32 = jnp.float32


def _cparams(sem):
    return pltpu.CompilerParams(dimension_semantics=sem, vmem_limit_bytes=V7X_VMEM_LIMIT_BYTES)


def _gelu_tanh(x):
    return 0.5 * x * (1.0 + jnp.tanh(0.7978845608028654 * (x + 0.044715 * (x * x * x))))


def _silu(x):
    return x * jax.nn.sigmoid(x)


def _modnorm(x, g, shift, scale):
    y = x * lax.rsqrt(jnp.mean(x * x, axis=-1, keepdims=True) + EPS)
    return (y * g) * (1.0 + scale) + shift


def _pick(n, pref):
    for t in pref:
        if n % t == 0:
            return t
    return n


def _ada_kernel(c_ref, w_ref, b_ref, o_ref):
    sc = _silu(c_ref[...]).astype(_BF16)
    o_ref[0] = jnp.dot(sc, w_ref[0].astype(_BF16), preferred_element_type=_F32) + b_ref[0]


def _adaln_all(c, w_ada, b_ada):
    ns, d, d3 = w_ada.shape
    b = c.shape[0]
    tn = _pick(d3, ADA_COL_TILES)
    return pl.pallas_call(
        _ada_kernel,
        grid=(ns, d3 // tn),
        in_specs=[
            pl.BlockSpec((b, d), lambda s, j: (0, 0)),
            pl.BlockSpec((1, d, tn), lambda s, j: (s, 0, j)),
            pl.BlockSpec((1, 1, tn), lambda s, j: (s, 0, j)),
        ],
        out_specs=pl.BlockSpec((1, b, tn), lambda s, j: (s, 0, j)),
        out_shape=jax.ShapeDtypeStruct((ns, b, d3), _F32),
        compiler_params=_cparams(("arbitrary", "arbitrary")),
        name="adaln_mod",
    )(c, w_ada, b_ada)


def _inproj_kernel(x_ref, mod_ref, g_ref, w_ref, lng_ref, lnb_ref, ws_ref, bst_ref, wp_ref, ps_ref,
                   a_ref, b_ref, gates_ref,
                   h_ref, u_ref, z_ref, vn_ref, halo_ref,
                   *, tiles_per_seq, heads, chunk, windows):
    i = pl.program_id(0)
    j = pl.program_id(1)
    tm = x_ref.shape[0]
    tn = w_ref.shape[1]
    sw = min(V7X_MXU_COLS, tn)

    def zslab(lo, width):
        return jnp.dot(h_ref[...], w_ref[:, lo:lo + width], preferred_element_type=_F32)

    def sgu_epilogue():
        rsum = jnp.zeros((tm, 1), _F32)
        for lo in range(0, tn, sw):
            rsum = rsum + jnp.sum(z_ref[:, lo:lo + sw], axis=-1, keepdims=True)
        mu = rsum * (1.0 / tn)
        vsum = jnp.zeros((tm, 1), _F32)
        for lo in range(0, tn, sw):
            vc = z_ref[:, lo:lo + sw] - mu
            vsum = vsum + jnp.sum(vc * vc, axis=-1, keepdims=True)
        rstd = lax.rsqrt(vsum * (1.0 / tn) + EPS)
        for lo in range(0, tn, sw):
            cs = slice(lo, lo + sw)
            vn_ref[:, cs] = ((z_ref[:, cs] - mu) * rstd * lng_ref[:, cs] + lnb_ref[:, cs]).astype(_BF16)
        hd = tn // heads
        row = lax.broadcasted_iota(jnp.int32, (chunk, chunk), 0)
        col = lax.broadcasted_iota(jnp.int32, (chunk, chunk), 1)
        causal = col <= row
        for hh in range(heads):
            wm = jnp.where(causal, ws_ref[hh], 0.0).astype(_BF16)
            bias = bst_ref[:, hh:hh + 1]
            for cc in range(tm // chunk):
                rs = slice(cc * chunk, (cc + 1) * chunk)
                cs = slice(hh * hd, (hh + 1) * hd)
                mixed = jnp.dot(wm, vn_ref[rs, cs], preferred_element_type=_F32) + bias
                a_ref[rs, cs] = (u_ref[rs, cs].astype(_F32) * mixed).astype(a_ref.dtype)

    def pool_epilogue():
        first = (i % tiles_per_seq) == 0
        pos = (i % tiles_per_seq) * tm + lax.broadcasted_iota(jnp.int32, (tm, 1), 0) + 1
        gd = tn // len(windows)
        for gi, win in enumerate(windows):
            cs = slice(gi * gd, (gi + 1) * gd)
            z = z_ref[:, cs]
            hist = jnp.where(first, 0.0, halo_ref[:, cs])
            acc = jnp.concatenate([hist, z], axis=0)
            halo_ref[:, cs] = z[tm - POOL_HALO:, :]
            step = 1
            while step < win:
                acc = acc + pltpu.roll(acc, step, 0)
                step *= 2
            cnt = jnp.minimum(pos, win).astype(_F32)
            pooled = acc[POOL_HALO:, :] / cnt - z
            y = jnp.dot(pooled.astype(_BF16), wp_ref[gi], preferred_element_type=_F32)
            b_ref[:, cs] = (y * ps_ref[:, cs]).astype(b_ref.dtype)

    @pl.when(j == 0)
    def _():
        m = mod_ref[0]
        h_ref[...] = _modnorm(x_ref[...], g_ref[...], m[0:1, :], m[1:2, :]).astype(_BF16)
        for lo in range(0, tn, sw):
            u_ref[:, lo:lo + sw] = _gelu_tanh(zslab(lo, sw)).astype(u_ref.dtype)

    @pl.when(j == 1)
    def _():
        for lo in range(0, tn, sw):
            z_ref[:, lo:lo + sw] = _gelu_tanh(zslab(lo, sw))
        sgu_epilogue()

    @pl.when(j == 2)
    def _():
        for lo in range(0, tn, sw):
            z_ref[:, lo:lo + sw] = zslab(lo, sw)
        pool_epilogue()

    @pl.when(j >= 3)
    def _():
        for lo in range(0, tn, sw):
            gates_ref[:, lo:lo + sw] = jax.nn.sigmoid(zslab(lo, sw)).astype(gates_ref.dtype)


def _inproj(x2, mod, norm_g, w_in, ln_g, ln_b, w_s, b_s_t, w_pool, pool_scale, seq):
    t, d = x2.shape
    d_in = w_in.shape[1]
    tn = ln_g.shape[-1]
    heads, chunk, _ = w_s.shape
    groups, gd, _ = w_pool.shape
    assert pool_scale.shape[-1] == tn and groups * gd == tn and d % tn == 0
    assert d_in == 3 * tn + 2 * d and max(POOL_WINDOWS[:groups]) <= POOL_HALO
    tm = _pick(seq, ROW_TILES_INPROJ)
    assert tm % chunk == 0 and tm >= POOL_HALO
    kern = functools.partial(_inproj_kernel, tiles_per_seq=seq // tm, heads=heads, chunk=chunk,
                             windows=POOL_WINDOWS[:groups])
    tps = seq // tm
    const2 = lambda i, j: (0, 0)
    const3 = lambda i, j: (0, 0, 0)
    return pl.pallas_call(
        kern,
        grid=(t // tm, d_in // tn),
        in_specs=[
            pl.BlockSpec((tm, d), lambda i, j: (i, 0)),
            pl.BlockSpec((1, 3, d), lambda i, j: (i // tps, 0, 0)),
            pl.BlockSpec((1, d), const2),
            pl.BlockSpec((d, tn), lambda i, j: (0, j)),
            pl.BlockSpec((1, tn), const2),
            pl.BlockSpec((1, tn), const2),
            pl.BlockSpec((heads, chunk, chunk), const3),
            pl.BlockSpec((chunk, heads), const2),
            pl.BlockSpec((groups, gd, gd), const3),
            pl.BlockSpec((1, tn), const2),
        ],
        out_specs=[
            pl.BlockSpec((tm, tn), lambda i, j: (i, 0)),
            pl.BlockSpec((tm, tn), lambda i, j: (i, 0)),
            pl.BlockSpec((tm, tn), lambda i, j: (i, jnp.maximum(j - 3, 0))),
        ],
        out_shape=[
            jax.ShapeDtypeStruct((t, tn), _BF16),
            jax.ShapeDtypeStruct((t, tn), _BF16),
            jax.ShapeDtypeStruct((t, 2 * d), _BF16),
        ],
        scratch_shapes=[
            pltpu.VMEM((tm, d), _BF16),
            pltpu.VMEM((tm, tn), _BF16),
            pltpu.VMEM((tm, tn), _F32),
            pltpu.VMEM((tm, tn), _BF16),
            pltpu.VMEM((POOL_HALO, tn), _F32),
        ],
        compiler_params=_cparams(("arbitrary", "arbitrary")),
        name="mixer_inproj",
    )(x2, mod, norm_g, w_in, ln_g, ln_b, w_s, b_s_t, w_pool, pool_scale)


def _outproj_kernel(a_ref, b_ref, sga_ref, sgb_ref, x_ref, mod_ref, woa_ref, wob_ref, wout_ref,
                    o_ref, m_ref, *, tc):
    d = x_ref.shape[1]
    for c in range(d // tc):
        cs = slice(c * tc, (c + 1) * tc)
        ya = jnp.dot(a_ref[...], woa_ref[:, cs], preferred_element_type=_F32)
        yb = jnp.dot(b_ref[...], wob_ref[:, cs], preferred_element_type=_F32)
        m_ref[:, cs] = (sga_ref[:, cs].astype(_F32) * ya + sgb_ref[:, cs].astype(_F32) * yb).astype(_BF16)
    gate = mod_ref[0][2:3, :]
    for c in range(d // tc):
        cs = slice(c * tc, (c + 1) * tc)
        y = jnp.dot(m_ref[...], wout_ref[:, cs], preferred_element_type=_F32)
        o_ref[:, cs] = x_ref[:, cs] + gate[:, cs] * y


def _outproj(a, b, gates, x2, mod, w_oa, w_ob, w_out, seq):
    t, d = x2.shape
    ds = a.shape[1]
    tm = _pick(seq, ROW_TILES)
    tc = _pick(d, OUT_COL_TILES)
    tps = seq // tm
    once = pl.Buffered(1)
    return pl.pallas_call(
        functools.partial(_outproj_kernel, tc=tc),
        grid=(t // tm,),
        in_specs=[
            pl.BlockSpec((tm, ds), lambda i: (i, 0)),
            pl.BlockSpec((tm, ds), lambda i: (i, 0)),
            pl.BlockSpec((tm, d), lambda i: (i, 0)),
            pl.BlockSpec((tm, d), lambda i: (i, 1)),
            pl.BlockSpec((tm, d), lambda i: (i, 0)),
            pl.BlockSpec((1, 3, d), lambda i: (i // tps, 0, 0)),
            pl.BlockSpec((ds, d), lambda i: (0, 0), pipeline_mode=once),
            pl.BlockSpec((ds, d), lambda i: (0, 0), pipeline_mode=once),
            pl.BlockSpec((d, d), lambda i: (0, 0), pipeline_mode=once),
        ],
        out_specs=pl.BlockSpec((tm, d), lambda i: (i, 0)),
        out_shape=jax.ShapeDtypeStruct((t, d), _F32),
        scratch_shapes=[pltpu.VMEM((tm, d), _BF16)],
        compiler_params=_cparams(("arbitrary",)),
        name="mixer_outproj",
    )(a, b, gates, gates, x2, mod, w_oa, w_ob, w_out)


def _swiglu_tile(h, wg, wu, wd):
    gte = jnp.dot(h, wg, preferred_element_type=_F32)
    up = jnp.dot(h, wu, preferred_element_type=_F32)
    act = (_silu(gte) * up).astype(_BF16)
    return jnp.dot(act, wd, preferred_element_type=_F32)


def _ffn_kernel(*refs, n_cast, cast_k, norm_k):
    x_ref, mod_ref, mcur_ref, g_ref, w1_ref, w3_ref, w2_ref = refs[:7]
    cast_in = refs[7:7 + n_cast]
    o_ref = refs[7 + n_cast]
    cast_out = refs[8 + n_cast:8 + 2 * n_cast]
    h_ref, hn_ref = refs[8 + 2 * n_cast:]
    i = pl.program_id(0)
    k = pl.program_id(1)

    def norm_next():
        m = mod_ref[0]
        hn_ref[...] = _modnorm(x_ref[...], g_ref[...], m[0:1, :], m[1:2, :]).astype(_BF16)

    def gated_tile(lhs_ref):
        return mcur_ref[0][2:3, :] * _swiglu_tile(lhs_ref[...], w1_ref[...], w3_ref[...], w2_ref[...])

    def cast_blocks():
        for src, dst in zip(cast_in, cast_out):
            dst[...] = src[...].astype(dst.dtype)

    @pl.when((i == 0) & (k == 0))
    def _():
        norm_next()

    @pl.when(k == 0)
    def _():
        o_ref[...] = x_ref[...] + gated_tile(hn_ref)
        h_ref[...] = hn_ref[...]
        if cast_k > 0:
            cast_blocks()

    @pl.when((k > 0) & (k < cast_k))
    def _():
        o_ref[...] += gated_tile(h_ref)
        cast_blocks()

    @pl.when(k == norm_k)
    def _():
        o_ref[...] += gated_tile(h_ref)
        norm_next()

    @pl.when((k >= max(cast_k, 1)) & (k != norm_k))
    def _():
        o_ref[...] += gated_tile(h_ref)


def _cast_plan(arrays, n_i, nk):
    for c in range(max(nk - 2, 0), 0, -1):
        if all(a.shape[0] % (n_i * c) == 0 and (a.shape[0] // (n_i * c)) % BF16_SUBLANE_TILE == 0 for a in arrays):
            return c
    return None


def _dense_ffn(x2, mod, norm_g, w1, w3, w2, seq, cast_srcs=()):
    t, d = x2.shape
    f = w1.shape[1]
    tm = _pick(seq, ROW_TILES)
    tf = _pick(f, HIDDEN_TILES_DENSE)
    tps = seq // tm
    n_i, nk = t // tm, f // tf
    srcs2d = [s.reshape(-1, s.shape[-1]) for s in cast_srcs]
    cast_k = _cast_plan(srcs2d, n_i, nk) if srcs2d else None
    if cast_k is None:
        srcs2d, cast_k = [], 0
    norm_k = nk - 1 if cast_k == 0 else max(cast_k, nk - 2)
    assert nk >= 2 and norm_k >= cast_k

    def x_tile(i, k):
        return jnp.minimum(i + (k >= norm_k).astype(jnp.int32), n_i - 1)

    def cast_block(i, k):
        return (i * cast_k + jnp.minimum(k, cast_k - 1), 0)

    cast_specs = [pl.BlockSpec((s.shape[0] // (n_i * cast_k), s.shape[1]), cast_block) for s in srcs2d]
    outs = pl.pallas_call(
        functools.partial(_ffn_kernel, n_cast=len(srcs2d), cast_k=cast_k, norm_k=norm_k),
        grid=(n_i, nk),
        in_specs=[
            pl.BlockSpec((tm, d), lambda i, k: (x_tile(i, k), 0)),
            pl.BlockSpec((1, 3, d), lambda i, k: (x_tile(i, k) // tps, 0, 0)),
            pl.BlockSpec((1, 3, d), lambda i, k: (i // tps, 0, 0)),
            pl.BlockSpec((1, d), lambda i, k: (0, 0)),
            pl.BlockSpec((d, tf), lambda i, k: (0, k)),
            pl.BlockSpec((d, tf), lambda i, k: (0, k)),
            pl.BlockSpec((tf, d), lambda i, k: (k, 0)),
        ] + cast_specs,
        out_specs=[pl.BlockSpec((tm, d), lambda i, k: (i, 0))] + cast_specs,
        out_shape=[jax.ShapeDtypeStruct((t, d), _F32)] + [jax.ShapeDtypeStruct(s.shape, _BF16) for s in srcs2d],
        scratch_shapes=[pltpu.VMEM((tm, d), _BF16), pltpu.VMEM((tm, d), _BF16)],
        compiler_params=_cparams(("arbitrary", "arbitrary")),
        name="dense_swiglu",
    )(x2, mod, mod, norm_g, w1, w3, w2, *srcs2d)
    if not srcs2d:
        return outs[0], None
    return outs[0], [o.reshape(s.shape) for o, s in zip(outs[1:], cast_srcs)]


def _route_kernel(x_ref, mod_ref, g_ref, wrt_ref, h_ref, route_ref, cnt_ref, run_ref):
    i = pl.program_id(0)
    tm = x_ref.shape[0]
    ne = wrt_ref.shape[0]

    @pl.when(i == 0)
    def _():
        run_ref[...] = jnp.zeros_like(run_ref)

    m = mod_ref[0]
    h = _modnorm(x_ref[...], g_ref[...], m[0:1, :], m[1:2, :])
    h_ref[...] = h
    lt = lax.dot_general(wrt_ref[...], h.astype(_BF16), (((1,), (1,)), ((), ())),
                         preferred_element_type=_F32)
    eid = lax.broadcasted_iota(jnp.int32, (ne, tm), 0).astype(_F32)
    m1 = jnp.max(lt, axis=0, keepdims=True)
    i1 = jnp.min(jnp.where(lt == m1, eid, float(ne)), axis=0, keepdims=True)
    l2 = jnp.where(eid == i1, -jnp.inf, lt)
    m2 = jnp.max(l2, axis=0, keepdims=True)
    i2 = jnp.min(jnp.where(l2 == m2, eid, float(ne)), axis=0, keepdims=True)
    e2 = jnp.exp(m2 - m1)
    g1 = 1.0 / (1.0 + e2)
    g2 = e2 / (1.0 + e2)
    oh1 = eid == i1
    oh2 = eid == i2
    both = jnp.where(oh1 | oh2, 1.0, 0.0).astype(_BF16)
    r = lax.broadcasted_iota(jnp.int32, (tm, tm), 0)
    c = lax.broadcasted_iota(jnp.int32, (tm, tm), 1)
    upper = jnp.where(r <= c, 1.0, 0.0).astype(_BF16)
    pref = jnp.dot(both, upper, preferred_element_type=_F32)
    base = run_ref[:, 0:1] + pref - 1.0
    rank1 = jnp.sum(jnp.where(oh1, base, 0.0), axis=0, keepdims=True)
    rank2 = jnp.sum(jnp.where(oh2, base, 0.0), axis=0, keepdims=True)
    run_new = run_ref[:, 0:1] + pref[:, tm - 1:tm]
    run_ref[...] = jnp.broadcast_to(run_new, run_ref.shape)
    cnt_ref[...] = jnp.broadcast_to(run_new, cnt_ref.shape)
    route_ref[0:1, :] = i1
    route_ref[1:2, :] = i2
    route_ref[2:3, :] = rank1
    route_ref[3:4, :] = rank2
    route_ref[4:5, :] = g1
    route_ref[5:6, :] = g2
    route_ref[6:8, :] = jnp.zeros((2, tm), _F32)


def _route(x2, mod, norm_g, w_router_t, seq):
    t, d = x2.shape
    ne = w_router_t.shape[0]
    tm = _pick(seq, ROW_TILES)
    tps = seq // tm
    return pl.pallas_call(
        _route_kernel,
        grid=(t // tm,),
        in_specs=[
            pl.BlockSpec((tm, d), lambda i: (i, 0)),
            pl.BlockSpec((1, 3, d), lambda i: (i // tps, 0, 0)),
            pl.BlockSpec((1, d), lambda i: (0, 0)),
            pl.BlockSpec((ne, d), lambda i: (0, 0)),
        ],
        out_specs=[
            pl.BlockSpec((tm, d), lambda i: (i, 0)),
            pl.BlockSpec((8, tm), lambda i: (0, i)),
            pl.BlockSpec((ne, V7X_LANES), lambda i: (0, 0)),
        ],
        out_shape=[
            jax.ShapeDtypeStruct((t, d), _F32),
            jax.ShapeDtypeStruct((8, t), _F32),
            jax.ShapeDtypeStruct((ne, V7X_LANES), _F32),
        ],
        scratch_shapes=[pltpu.VMEM((ne, V7X_LANES), _F32)],
        compiler_params=_cparams(("arbitrary",)),
        name="moe_route",
    )(x2, mod, norm_g, w_router_t)


def _row_copy(src_ref, src_row, dst_ref, dst_row, sem):
    return pltpu.make_async_copy(src_ref.at[pl.ds(src_row, 1)], dst_ref.at[pl.ds(dst_row, 1)], sem)


def _dispatch_kernel(dest_ref, hole_ref, h_ref, xs_ref, zero_ref, sem):
    tm = h_ref.shape[0]
    n_holes = hole_ref.shape[-1]
    zero_ref[...] = jnp.zeros_like(zero_ref)

    def issue(r, carry):
        for kk in range(TOP_K):
            _row_copy(h_ref, r, xs_ref, dest_ref[0, 0, kk * tm + r], sem).start(priority=kk % 2)
        return carry

    lax.fori_loop(0, tm, issue, 0, unroll=DMA_ISSUE_UNROLL)

    def fill(r, carry):
        _row_copy(zero_ref, 0, xs_ref, hole_ref[0, 0, r], sem).start()
        return carry

    lax.fori_loop(0, n_holes, fill, 0, unroll=DMA_ISSUE_UNROLL)
    for kk in range(TOP_K):
        pltpu.make_async_copy(h_ref, xs_ref.at[pl.ds(0, tm)], sem).wait()
    pltpu.make_async_copy(h_ref.at[pl.ds(0, n_holes)], xs_ref.at[pl.ds(0, n_holes)], sem).wait()


def _dispatch(h, dest_tiles, hole_tiles, n_rows):
    t, d = h.shape
    tm = dest_tiles.shape[-1] // TOP_K
    n_holes = hole_tiles.shape[-1]
    assert n_holes <= tm
    return pl.pallas_call(
        _dispatch_kernel,
        grid=(t // tm,),
        in_specs=[
            pl.BlockSpec((1, 1, TOP_K * tm), lambda i: (i, 0, 0), memory_space=pltpu.SMEM),
            pl.BlockSpec((1, 1, n_holes), lambda i: (i, 0, 0), memory_space=pltpu.SMEM),
            pl.BlockSpec((tm, d), lambda i: (i, 0)),
        ],
        out_specs=pl.BlockSpec(memory_space=pl.ANY),
        out_shape=jax.ShapeDtypeStruct((n_rows, d), _F32),
        scratch_shapes=[pltpu.VMEM((8, d), _F32), pltpu.SemaphoreType.DMA(())],
        compiler_params=_cparams(("arbitrary",)),
        name="moe_dispatch",
    )(dest_tiles, hole_tiles, h)


def _combine_kernel(dest_ref, dest_next_ref, ys_ref, gates_ref, x_ref, mod_ref, fg_ref, o_ref, buf_ref, sems,
                    *, final_norm):
    i = pl.program_id(0)
    tm = x_ref.shape[0]
    slot = i % 2

    def gather(idx_ref, s):
        def issue(r, carry):
            for kk in range(TOP_K):
                _row_copy(ys_ref, idx_ref[0, 0, kk * tm + r], buf_ref.at[s, kk], r, sems.at[s]).start(priority=kk % 2)
            return carry

        lax.fori_loop(0, tm, issue, 0, unroll=DMA_ISSUE_UNROLL)

    @pl.when(i == 0)
    def _():
        gather(dest_ref, 0)

    @pl.when(i + 1 < pl.num_programs(0))
    def _():
        gather(dest_next_ref, 1 - slot)

    for kk in range(TOP_K):
        pltpu.make_async_copy(ys_ref.at[pl.ds(0, tm)], buf_ref.at[slot, kk], sems.at[slot]).wait()
    y = gates_ref[:, 0:1] * buf_ref[slot, 0]
    for kk in range(1, TOP_K):
        y = y + gates_ref[:, kk:kk + 1] * buf_ref[slot, kk]
    xn = x_ref[...] + mod_ref[0][2:3, :] * y
    if final_norm:
        xn = xn * lax.rsqrt(jnp.mean(xn * xn, axis=-1, keepdims=True) + EPS) * fg_ref[...]
    o_ref[...] = xn


def _combine(ys, dest_tiles, gates, x2, mod, final_g, seq, final_norm):
    t, d = x2.shape
    tm = dest_tiles.shape[-1] // TOP_K
    tps = seq // tm
    n_i = t // tm
    return pl.pallas_call(
        functools.partial(_combine_kernel, final_norm=final_norm),
        grid=(n_i,),
        in_specs=[
            pl.BlockSpec((1, 1, TOP_K * tm), lambda i: (i, 0, 0), memory_space=pltpu.SMEM),
            pl.BlockSpec((1, 1, TOP_K * tm), lambda i: (jnp.minimum(i + 1, n_i - 1), 0, 0), memory_space=pltpu.SMEM),
            pl.BlockSpec(memory_space=pl.ANY),
            pl.BlockSpec((tm, TOP_K), lambda i: (i, 0)),
            pl.BlockSpec((tm, d), lambda i: (i, 0)),
            pl.BlockSpec((1, 3, d), lambda i: (i // tps, 0, 0)),
            pl.BlockSpec((1, d), lambda i: (0, 0)),
        ],
        out_specs=pl.BlockSpec((tm, d), lambda i: (i, 0)),
        out_shape=jax.ShapeDtypeStruct((t, d), _F32),
        scratch_shapes=[pltpu.VMEM((2, TOP_K, tm, d), _F32), pltpu.SemaphoreType.DMA((2,))],
        compiler_params=_cparams(("arbitrary",)),
        name="moe_combine",
    )(dest_tiles, dest_tiles, ys, gates, x2, mod, final_g)


def _experts_kernel(te_ref, nv_ref, xs_ref, wg_ref, wu_ref, wd_ref, o_ref, h_ref):
    i = pl.program_id(0)
    k = pl.program_id(1)

    @pl.when(k == 0)
    def _():
        o_ref[...] = jnp.zeros_like(o_ref)

    @pl.when(i < nv_ref[0])
    def _():
        @pl.when(k == 0)
        def _():
            h_ref[...] = xs_ref[...].astype(_BF16)

        o_ref[...] += _swiglu_tile(h_ref[...], wg_ref[0], wu_ref[0], wd_ref[0])


def _experts(xs, tile_expert, n_valid, w_gate, w_up, w_down, tm):
    r, d = xs.shape
    ne, _, f = w_gate.shape
    tf = _pick(f, HIDDEN_TILES_EXPERT)
    nk = f // tf
    n_tiles = r // tm

    def kk(i, k, nv):
        return jnp.where(i < nv[0], k, nk - 1)

    return pl.pallas_call(
        _experts_kernel,
        grid_spec=pltpu.PrefetchScalarGridSpec(
            num_scalar_prefetch=2,
            grid=(n_tiles, nk),
            in_specs=[
                pl.BlockSpec((tm, d), lambda i, k, te, nv: (jnp.minimum(i, nv[0] - 1), 0)),
                pl.BlockSpec((1, d, tf), lambda i, k, te, nv: (te[i], 0, kk(i, k, nv))),
                pl.BlockSpec((1, d, tf), lambda i, k, te, nv: (te[i], 0, kk(i, k, nv))),
                pl.BlockSpec((1, tf, d), lambda i, k, te, nv: (te[i], kk(i, k, nv), 0)),
            ],
            out_specs=pl.BlockSpec((tm, d), lambda i, k, te, nv: (i, 0)),
            scratch_shapes=[pltpu.VMEM((tm, d), _BF16)],
        ),
        out_shape=jax.ShapeDtypeStruct((r, d), _F32),
        compiler_params=_cparams(("arbitrary", "arbitrary")),
        name="moe_experts",
    )(tile_expert, n_valid, xs, w_gate, w_up, w_down)


def _moe_layer(x2, mod, norm_g, w_router, w_gate, w_up, w_down, final_g, seq, final_norm):
    t, d = x2.shape
    ne = w_router.shape[1]
    h, route, cnt = _route(x2, mod, norm_g, w_router.T.astype(_BF16), seq)
    tm = _pick(seq, ROW_TILES)
    counts = cnt[:, 0].astype(jnp.int32)
    padded = (counts + tm - 1) // tm * tm
    pend = jnp.cumsum(padded)
    pstart = pend - padded
    idx = route[0:TOP_K].astype(jnp.int32)
    rank = route[TOP_K:2 * TOP_K].astype(jnp.int32)
    dest = rank
    for e in range(ne):
        dest = dest + jnp.where(idx == e, pstart[e], 0)
    gates = route[2 * TOP_K:3 * TOP_K].T
    n_tiles = (TOP_K * t) // tm + ne
    dest_tiles = dest.reshape(TOP_K, t // tm, tm).transpose(1, 0, 2).reshape(t // tm, 1, TOP_K * tm)
    tile_start = jnp.arange(n_tiles, dtype=jnp.int32) * tm
    tile_expert = jnp.minimum(jnp.sum(pend[None, :] <= tile_start[:, None], axis=1), ne - 1).astype(jnp.int32)
    n_valid = (pend[-1] // tm).astype(jnp.int32).reshape(1)
    n_steps = t // tm
    holes_per_step = -(-(ne * tm) // n_steps)
    hid = jnp.minimum(jnp.arange(n_steps * holes_per_step, dtype=jnp.int32), ne * tm - 1)
    pad = padded - counts
    pad_end = jnp.cumsum(pad)
    hole = pend[-1] + (hid - pad_end[-1])
    for e in range(ne - 1, -1, -1):
        hole = jnp.where(hid < pad_end[e], pstart[e] + counts[e] + (hid - (pad_end[e] - pad[e])), hole)
    hole_tiles = hole.reshape(n_steps, 1, holes_per_step)
    xs = _dispatch(h, dest_tiles, hole_tiles, n_tiles * tm)
    ys = _experts(xs, tile_expert, n_valid, w_gate, w_up, w_down, tm)
    return _combine(ys, dest_tiles, gates, x2, mod, final_g, seq, final_norm)


def _final_norm_kernel(x_ref, g_ref, o_ref):
    x = x_ref[...]
    o_ref[...] = x * lax.rsqrt(jnp.mean(x * x, axis=-1, keepdims=True) + EPS) * g_ref[...]


def _final_norm(x2, fg):
    t, d = x2.shape
    tm = _pick(t, ROW_TILES)
    return pl.pallas_call(
        _final_norm_kernel,
        grid=(t // tm,),
        in_specs=[pl.BlockSpec((tm, d), lambda i: (i, 0)), pl.BlockSpec((1, d), lambda i: (0, 0))],
        out_specs=pl.BlockSpec((tm, d), lambda i: (i, 0)),
        out_shape=jax.ShapeDtypeStruct((t, d), _F32),
        compiler_params=_cparams(("arbitrary",)),
        name="final_norm",
    )(x2, fg)


def kernel(x, c, w_ada, b_ada, norm_g, w_in, ln_g, ln_b, w_s, b_s, w_pool, pool_scale, w_oa, w_ob, w_out,
           ffn_w1, ffn_w3, ffn_w2, w_router, moe_w_gate, moe_w_up, moe_w_down, final_g):
    bsz, seq, d = x.shape
    depth = w_in.shape[0]
    t = bsz * seq
    x2 = x.reshape(t, d)
    mods = _adaln_all(c, w_ada.reshape(depth * 2, d, 3 * d), b_ada.reshape(depth * 2, 1, 3 * d))
    mods = mods.reshape(depth * 2, bsz, 3, d)
    fg = final_g.reshape(1, d)
    bf = lambda w: w.astype(_BF16)
    moe_bf16 = {}
    for i in range(depth):
        last = i == depth - 1
        a, b, gates = _inproj(x2, mods[2 * i], norm_g[i, 0].reshape(1, d), bf(w_in[i]),
                              ln_g[i].reshape(1, -1), ln_b[i].reshape(1, -1), w_s[i], b_s[i].T,
                              bf(w_pool[i]), pool_scale[i].reshape(1, -1), seq)
        x2 = _outproj(a, b, gates, x2, mods[2 * i], bf(w_oa[i]), bf(w_ob[i]), bf(w_out[i]), seq)
        j = i // 2
        ng = norm_g[i, 1].reshape(1, d)
        if i % 2 == 0:
            nxt = (i + 1) // 2
            pending = (moe_w_gate[nxt], moe_w_up[nxt], moe_w_down[nxt]) if i + 1 < depth else ()
            x2, converted = _dense_ffn(x2, mods[2 * i + 1], ng, bf(ffn_w1[j]), bf(ffn_w3[j]), bf(ffn_w2[j]),
                                       seq, cast_srcs=pending)
            if converted is not None:
                moe_bf16[nxt] = converted
            if last:
                x2 = _final_norm(x2, fg)
        else:
            wg, wu, wd = moe_bf16.get(j) or (bf(moe_w_gate[j]), bf(moe_w_up[j]), bf(moe_w_down[j]))
            x2 = _moe_layer(x2, mods[2 * i + 1], ng, w_router[j], wg, wu, wd, fg, seq, final_norm=last)
    return x2.reshape(bsz, seq, d)
```

```python
import functools

import jax
import jax.numpy as jnp
from jax import lax
from jax.experimental import pallas as pl
from jax.experimental.pallas import tpu as pltpu

EPS = 1e-6
POOL_WINDOWS = (2, 4, 8, 16)
TOP_K = 2
POOL_HALO = 16
V7X_MXU_COLS = 256
V7X_VMEM_LIMIT_BYTES = 56 * 1024 * 1024
V7X_LANES = 128
BF16_SUBLANE_TILE = 16
DMA_ISSUE_UNROLL = 8
ROW_TILES_INPROJ = (1024, 512, 256, 128)
ROW_TILES = (512, 256, 128)
HIDDEN_TILES_DENSE = (512, 256, 128)
HIDDEN_TILES_EXPERT = (1024, 512, 256, 128)
OUT_COL_TILES = (512, 256, 128)
ADA_COL_TILES = (768, 512, 256, 128)

_BF16 = jnp.bfloat16
_F32 = jnp.float32


def _cparams(sem):
    return pltpu.CompilerParams(dimension_semantics=sem, vmem_limit_bytes=V7X_VMEM_LIMIT_BYTES)


def _gelu_tanh(x):
    return 0.5 * x * (1.0 + jnp.tanh(0.7978845608028654 * (x + 0.044715 * (x * x * x))))


def _silu(x):
    return x * jax.nn.sigmoid(x)


def _modnorm(x, g, shift, scale):
    y = x * lax.rsqrt(jnp.mean(x * x, axis=-1, keepdims=True) + EPS)
    return (y * g) * (1.0 + scale) + shift


def _pick(n, pref):
    for t in pref:
        if n % t == 0:
            return t
    return n


def _ada_kernel(c_ref, w_ref, b_ref, o_ref):
    sc = _silu(c_ref[...]).astype(_BF16)
    o_ref[0] = jnp.dot(sc, w_ref[0].astype(_BF16), preferred_element_type=_F32) + b_ref[0]


def _adaln_all(c, w_ada, b_ada):
    ns, d, d3 = w_ada.shape
    b = c.shape[0]
    tn = _pick(d3, ADA_COL_TILES)
    return pl.pallas_call(
        _ada_kernel,
        grid=(ns, d3 // tn),
        in_specs=[
            pl.BlockSpec((b, d), lambda s, j: (0, 0)),
            pl.BlockSpec((1, d, tn), lambda s, j: (s, 0, j)),
            pl.BlockSpec((1, 1, tn), lambda s, j: (s, 0, j)),
        ],
        out_specs=pl.BlockSpec((1, b, tn), lambda s, j: (s, 0, j)),
        out_shape=jax.ShapeDtypeStruct((ns, b, d3), _F32),
        compiler_params=_cparams(("arbitrary", "arbitrary")),
        name="adaln_mod",
    )(c, w_ada, b_ada)


def _inproj_kernel(x_ref, mod_ref, g_ref, w_ref, lng_ref, lnb_ref, ws_ref, bst_ref, wp_ref, ps_ref,
                   a_ref, b_ref, gates_ref,
                   h_ref, u_ref, z_ref, vn_ref, halo_ref,
                   *, tiles_per_seq, heads, chunk, windows):
    i = pl.program_id(0)
    j = pl.program_id(1)
    tm = x_ref.shape[0]
    tn = w_ref.shape[1]
    sw = min(V7X_MXU_COLS, tn)

    def zslab(lo, width):
        return jnp.dot(h_ref[...], w_ref[:, lo:lo + width], preferred_element_type=_F32)

    def sgu_epilogue():
        rsum = jnp.zeros((tm, 1), _F32)
        for lo in range(0, tn, sw):
            rsum = rsum + jnp.sum(z_ref[:, lo:lo + sw], axis=-1, keepdims=True)
        mu = rsum * (1.0 / tn)
        vsum = jnp.zeros((tm, 1), _F32)
        for lo in range(0, tn, sw):
            vc = z_ref[:, lo:lo + sw] - mu
            vsum = vsum + jnp.sum(vc * vc, axis=-1, keepdims=True)
        rstd = lax.rsqrt(vsum * (1.0 / tn) + EPS)
        for lo in range(0, tn, sw):
            cs = slice(lo, lo + sw)
            vn_ref[:, cs] = ((z_ref[:, cs] - mu) * rstd * lng_ref[:, cs] + lnb_ref[:, cs]).astype(_BF16)
        hd = tn // heads
        row = lax.broadcasted_iota(jnp.int32, (chunk, chunk), 0)
        col = lax.broadcasted_iota(jnp.int32, (chunk, chunk), 1)
        causal = col <= row
        for hh in range(heads):
            wm = jnp.where(causal, ws_ref[hh], 0.0).astype(_BF16)
            bias = bst_ref[:, hh:hh + 1]
            for cc in range(tm // chunk):
                rs = slice(cc * chunk, (cc + 1) * chunk)
                cs = slice(hh * hd, (hh + 1) * hd)
                mixed = jnp.dot(wm, vn_ref[rs, cs], preferred_element_type=_F32) + bias
                a_ref[rs, cs] = (u_ref[rs, cs].astype(_F32) * mixed).astype(a_ref.dtype)

    def pool_epilogue():
        first = (i % tiles_per_seq) == 0
        pos = (i % tiles_per_seq) * tm + lax.broadcasted_iota(jnp.int32, (tm, 1), 0) + 1
        gd = tn // len(windows)
        for gi, win in enumerate(windows):
            cs = slice(gi * gd, (gi + 1) * gd)
            z = z_ref[:, cs]
            hist = jnp.where(first, 0.0, halo_ref[:, cs])
            acc = jnp.concatenate([hist, z], axis=0)
            halo_ref[:, cs] = z[tm - POOL_HALO:, :]
            step = 1
            while step < win:
                acc = acc + pltpu.roll(acc, step, 0)
                step *= 2
            cnt = jnp.minimum(pos, win).astype(_F32)
            pooled = acc[POOL_HALO:, :] / cnt - z
            y = jnp.dot(pooled.astype(_BF16), wp_ref[gi], preferred_element_type=_F32)
            b_ref[:, cs] = (y * ps_ref[:, cs]).astype(b_ref.dtype)

    @pl.when(j == 0)
    def _():
        m = mod_ref[0]
        h_ref[...] = _modnorm(x_ref[...], g_ref[...], m[0:1, :], m[1:2, :]).astype(_BF16)
        for lo in range(0, tn, sw):
            u_ref[:, lo:lo + sw] = _gelu_tanh(zslab(lo, sw)).astype(u_ref.dtype)

    @pl.when(j == 1)
    def _():
        for lo in range(0, tn, sw):
            z_ref[:, lo:lo + sw] = _gelu_tanh(zslab(lo, sw))
        sgu_epilogue()

    @pl.when(j == 2)
    def _():
        for lo in range(0, tn, sw):
            z_ref[:, lo:lo + sw] = zslab(lo, sw)
        pool_epilogue()

    @pl.when(j >= 3)
    def _():
        for lo in range(0, tn, sw):
            gates_ref[:, lo:lo + sw] = jax.nn.sigmoid(zslab(lo, sw)).astype(gates_ref.dtype)


def _inproj(x2, mod, norm_g, w_in, ln_g, ln_b, w_s, b_s_t, w_pool, pool_scale, seq, layer):
    t, d = x2.shape
    d_in = w_in.shape[2]
    tn = ln_g.shape[-1]
    heads, chunk, _ = w_s.shape
    _, groups, gd, _ = w_pool.shape
    assert pool_scale.shape[-1] == tn and groups * gd == tn and d % tn == 0
    assert d_in == 3 * tn + 2 * d and max(POOL_WINDOWS[:groups]) <= POOL_HALO
    tm = _pick(seq, ROW_TILES_INPROJ)
    assert tm % chunk == 0 and tm >= POOL_HALO
    kern = functools.partial(_inproj_kernel, tiles_per_seq=seq // tm, heads=heads, chunk=chunk,
                             windows=POOL_WINDOWS[:groups])
    tps = seq // tm
    const2 = lambda i, j: (0, 0)
    const3 = lambda i, j: (0, 0, 0)
    return pl.pallas_call(
        kern,
        grid=(t // tm, d_in // tn),
        in_specs=[
            pl.BlockSpec((tm, d), lambda i, j: (i, 0)),
            pl.BlockSpec((1, 3, d), lambda i, j: (i // tps, 0, 0)),
            pl.BlockSpec((1, d), const2),
            pl.BlockSpec((None, d, tn), lambda i, j: (layer, 0, j)),
            pl.BlockSpec((1, tn), const2),
            pl.BlockSpec((1, tn), const2),
            pl.BlockSpec((heads, chunk, chunk), const3),
            pl.BlockSpec((chunk, heads), const2),
            pl.BlockSpec((None, groups, gd, gd), lambda i, j: (layer, 0, 0, 0)),
            pl.BlockSpec((1, tn), const2),
        ],
        out_specs=[
            pl.BlockSpec((tm, tn), lambda i, j: (i, 0)),
            pl.BlockSpec((tm, tn), lambda i, j: (i, 0)),
            pl.BlockSpec((tm, tn), lambda i, j: (i, jnp.maximum(j - 3, 0))),
        ],
        out_shape=[
            jax.ShapeDtypeStruct((t, tn), _BF16),
            jax.ShapeDtypeStruct((t, tn), _BF16),
            jax.ShapeDtypeStruct((t, 2 * d), _BF16),
        ],
        scratch_shapes=[
            pltpu.VMEM((tm, d), _BF16),
            pltpu.VMEM((tm, tn), _BF16),
            pltpu.VMEM((tm, tn), _F32),
            pltpu.VMEM((tm, tn), _BF16),
            pltpu.VMEM((POOL_HALO, tn), _F32),
        ],
        compiler_params=_cparams(("arbitrary", "arbitrary")),
        name="mixer_inproj",
    )(x2, mod, norm_g, w_in, ln_g, ln_b, w_s, b_s_t, w_pool, pool_scale)


def _outproj_kernel(a_ref, b_ref, sga_ref, sgb_ref, x_ref, mod_ref, woa_ref, wob_ref, wout_ref,
                    o_ref, m_ref, *, tc):
    d = x_ref.shape[1]
    for c in range(d // tc):
        cs = slice(c * tc, (c + 1) * tc)
        ya = jnp.dot(a_ref[...], woa_ref[:, cs], preferred_element_type=_F32)
        yb = jnp.dot(b_ref[...], wob_ref[:, cs], preferred_element_type=_F32)
        m_ref[:, cs] = (sga_ref[:, cs].astype(_F32) * ya + sgb_ref[:, cs].astype(_F32) * yb).astype(_BF16)
    gate = mod_ref[0][2:3, :]
    for c in range(d // tc):
        cs = slice(c * tc, (c + 1) * tc)
        y = jnp.dot(m_ref[...], wout_ref[:, cs], preferred_element_type=_F32)
        o_ref[:, cs] = x_ref[:, cs] + gate[:, cs] * y


def _outproj(a, b, gates, x2, mod, w_oa, w_ob, w_out, seq, layer):
    t, d = x2.shape
    ds = a.shape[1]
    tm = _pick(seq, ROW_TILES)
    tc = _pick(d, OUT_COL_TILES)
    tps = seq // tm
    once = pl.Buffered(1)
    return pl.pallas_call(
        functools.partial(_outproj_kernel, tc=tc),
        grid=(t // tm,),
        in_specs=[
            pl.BlockSpec((tm, ds), lambda i: (i, 0)),
            pl.BlockSpec((tm, ds), lambda i: (i, 0)),
            pl.BlockSpec((tm, d), lambda i: (i, 0)),
            pl.BlockSpec((tm, d), lambda i: (i, 1)),
            pl.BlockSpec((tm, d), lambda i: (i, 0)),
            pl.BlockSpec((1, 3, d), lambda i: (i // tps, 0, 0)),
            pl.BlockSpec((None, ds, d), lambda i: (layer, 0, 0), pipeline_mode=once),
            pl.BlockSpec((None, ds, d), lambda i: (layer, 0, 0), pipeline_mode=once),
            pl.BlockSpec((None, d, d), lambda i: (layer, 0, 0), pipeline_mode=once),
        ],
        out_specs=pl.BlockSpec((tm, d), lambda i: (i, 0)),
        out_shape=jax.ShapeDtypeStruct((t, d), _F32),
        scratch_shapes=[pltpu.VMEM((tm, d), _BF16)],
        compiler_params=_cparams(("arbitrary",)),
        name="mixer_outproj",
    )(a, b, gates, gates, x2, mod, w_oa, w_ob, w_out)


def _swiglu_tile(h, wg, wu, wd):
    gte = jnp.dot(h, wg, preferred_element_type=_F32)
    up = jnp.dot(h, wu, preferred_element_type=_F32)
    act = (_silu(gte) * up).astype(_BF16)
    return jnp.dot(act, wd, preferred_element_type=_F32)


def _ffn_kernel(*refs, n_cast, cast_k, norm_k):
    x_ref, mod_ref, mcur_ref, g_ref, w1_ref, w3_ref, w2_ref = refs[:7]
    cast_in = refs[7:7 + n_cast]
    o_ref = refs[7 + n_cast]
    cast_out = refs[8 + n_cast:8 + 2 * n_cast]
    h_ref, hn_ref = refs[8 + 2 * n_cast:]
    i = pl.program_id(0)
    k = pl.program_id(1)

    def norm_next():
        m = mod_ref[0]
        hn_ref[...] = _modnorm(x_ref[...], g_ref[...], m[0:1, :], m[1:2, :]).astype(_BF16)

    def gated_tile(lhs_ref):
        return mcur_ref[0][2:3, :] * _swiglu_tile(lhs_ref[...], w1_ref[...], w3_ref[...], w2_ref[...])

    def cast_blocks():
        for src, dst in zip(cast_in, cast_out):
            dst[...] = src[...].astype(dst.dtype)

    @pl.when((i == 0) & (k == 0))
    def _():
        norm_next()

    @pl.when(k == 0)
    def _():
        o_ref[...] = x_ref[...] + gated_tile(hn_ref)
        h_ref[...] = hn_ref[...]
        if cast_k > 0:
            cast_blocks()

    @pl.when((k > 0) & (k < cast_k))
    def _():
        o_ref[...] += gated_tile(h_ref)
        cast_blocks()

    @pl.when(k == norm_k)
    def _():
        o_ref[...] += gated_tile(h_ref)
        norm_next()

    @pl.when((k >= max(cast_k, 1)) & (k != norm_k))
    def _():
        o_ref[...] += gated_tile(h_ref)


def _cast_plan(arrays, n_i, nk):
    for c in range(max(nk - 2, 0), 0, -1):
        if all(a.shape[0] % (n_i * c) == 0 and (a.shape[0] // (n_i * c)) % BF16_SUBLANE_TILE == 0 for a in arrays):
            return c
    return None


def _dense_ffn(x2, mod, norm_g, w1, w3, w2, seq, cast_srcs=()):
    t, d = x2.shape
    f = w1.shape[1]
    tm = _pick(seq, ROW_TILES)
    tf = _pick(f, HIDDEN_TILES_DENSE)
    tps = seq // tm
    n_i, nk = t // tm, f // tf
    srcs2d = [s.reshape(-1, s.shape[-1]) for s in cast_srcs]
    cast_k = _cast_plan(srcs2d, n_i, nk) if srcs2d else None
    if cast_k is None:
        srcs2d, cast_k = [], 0
    norm_k = nk - 1 if cast_k == 0 else max(cast_k, nk - 2)
    assert nk >= 2 and norm_k >= cast_k

    def x_tile(i, k):
        return jnp.minimum(i + (k >= norm_k).astype(jnp.int32), n_i - 1)

    def cast_block(i, k):
        return (i * cast_k + jnp.minimum(k, cast_k - 1), 0)

    cast_specs = [pl.BlockSpec((s.shape[0] // (n_i * cast_k), s.shape[1]), cast_block) for s in srcs2d]
    outs = pl.pallas_call(
        functools.partial(_ffn_kernel, n_cast=len(srcs2d), cast_k=cast_k, norm_k=norm_k),
        grid=(n_i, nk),
        in_specs=[
            pl.BlockSpec((tm, d), lambda i, k: (x_tile(i, k), 0)),
            pl.BlockSpec((1, 3, d), lambda i, k: (x_tile(i, k) // tps, 0, 0)),
            pl.BlockSpec((1, 3, d), lambda i, k: (i // tps, 0, 0)),
            pl.BlockSpec((1, d), lambda i, k: (0, 0)),
            pl.BlockSpec((d, tf), lambda i, k: (0, k)),
            pl.BlockSpec((d, tf), lambda i, k: (0, k)),
            pl.BlockSpec((tf, d), lambda i, k: (k, 0)),
        ] + cast_specs,
        out_specs=[pl.BlockSpec((tm, d), lambda i, k: (i, 0))] + cast_specs,
        out_shape=[jax.ShapeDtypeStruct((t, d), _F32)] + [jax.ShapeDtypeStruct(s.shape, _BF16) for s in srcs2d],
        scratch_shapes=[pltpu.VMEM((tm, d), _BF16), pltpu.VMEM((tm, d), _BF16)],
        compiler_params=_cparams(("arbitrary", "arbitrary")),
        name="dense_swiglu",
    )(x2, mod, mod, norm_g, w1, w3, w2, *srcs2d)
    if not srcs2d:
        return outs[0], None
    return outs[0], [o.reshape(s.shape) for o, s in zip(outs[1:], cast_srcs)]


def _route_kernel(x_ref, mod_ref, g_ref, wrt_ref, h_ref, route_ref, cnt_ref, run_ref):
    i = pl.program_id(0)
    tm = x_ref.shape[0]
    ne = wrt_ref.shape[0]

    @pl.when(i == 0)
    def _():
        run_ref[...] = jnp.zeros_like(run_ref)

    m = mod_ref[0]
    h = _modnorm(x_ref[...], g_ref[...], m[0:1, :], m[1:2, :])
    h_ref[...] = h
    lt = lax.dot_general(wrt_ref[...], h.astype(_BF16), (((1,), (1,)), ((), ())),
                         preferred_element_type=_F32)
    eid = lax.broadcasted_iota(jnp.int32, (ne, tm), 0).astype(_F32)
    m1 = jnp.max(lt, axis=0, keepdims=True)
    i1 = jnp.min(jnp.where(lt == m1, eid, float(ne)), axis=0, keepdims=True)
    l2 = jnp.where(eid == i1, -jnp.inf, lt)
    m2 = jnp.max(l2, axis=0, keepdims=True)
    i2 = jnp.min(jnp.where(l2 == m2, eid, float(ne)), axis=0, keepdims=True)
    e2 = jnp.exp(m2 - m1)
    g1 = 1.0 / (1.0 + e2)
    g2 = e2 / (1.0 + e2)
    oh1 = eid == i1
    oh2 = eid == i2
    both = jnp.where(oh1 | oh2, 1.0, 0.0).astype(_BF16)
    r = lax.broadcasted_iota(jnp.int32, (tm, tm), 0)
    c = lax.broadcasted_iota(jnp.int32, (tm, tm), 1)
    upper = jnp.where(r <= c, 1.0, 0.0).astype(_BF16)
    pref = jnp.dot(both, upper, preferred_element_type=_F32)
    base = run_ref[:, 0:1] + pref - 1.0
    rank1 = jnp.sum(jnp.where(oh1, base, 0.0), axis=0, keepdims=True)
    rank2 = jnp.sum(jnp.where(oh2, base, 0.0), axis=0, keepdims=True)
    run_new = run_ref[:, 0:1] + pref[:, tm - 1:tm]
    run_ref[...] = jnp.broadcast_to(run_new, run_ref.shape)
    cnt_ref[...] = jnp.broadcast_to(run_new, cnt_ref.shape)
    route_ref[0:1, :] = i1
    route_ref[1:2, :] = i2
    route_ref[2:3, :] = rank1
    route_ref[3:4, :] = rank2
    route_ref[4:5, :] = g1
    route_ref[5:6, :] = g2
    route_ref[6:8, :] = jnp.zeros((2, tm), _F32)


def _route(x2, mod, norm_g, w_router_t, seq):
    t, d = x2.shape
    ne = w_router_t.shape[0]
    tm = _pick(seq, ROW_TILES)
    tps = seq // tm
    return pl.pallas_call(
        _route_kernel,
        grid=(t // tm,),
        in_specs=[
            pl.BlockSpec((tm, d), lambda i: (i, 0)),
            pl.BlockSpec((1, 3, d), lambda i: (i // tps, 0, 0)),
            pl.BlockSpec((1, d), lambda i: (0, 0)),
            pl.BlockSpec((ne, d), lambda i: (0, 0)),
        ],
        out_specs=[
            pl.BlockSpec((tm, d), lambda i: (i, 0)),
            pl.BlockSpec((8, tm), lambda i: (0, i)),
            pl.BlockSpec((ne, V7X_LANES), lambda i: (0, 0)),
        ],
        out_shape=[
            jax.ShapeDtypeStruct((t, d), _F32),
            jax.ShapeDtypeStruct((8, t), _F32),
            jax.ShapeDtypeStruct((ne, V7X_LANES), _F32),
        ],
        scratch_shapes=[pltpu.VMEM((ne, V7X_LANES), _F32)],
        compiler_params=_cparams(("arbitrary",)),
        name="moe_route",
    )(x2, mod, norm_g, w_router_t)


def _row_copy(src_ref, src_row, dst_ref, dst_row, sem):
    return pltpu.make_async_copy(src_ref.at[pl.ds(src_row, 1)], dst_ref.at[pl.ds(dst_row, 1)], sem)


def _dispatch_kernel(dest_ref, hole_ref, h_ref, xs_ref, zero_ref, sem):
    tm = h_ref.shape[0]
    n_holes = hole_ref.shape[-1]
    zero_ref[...] = jnp.zeros_like(zero_ref)

    def issue(r, carry):
        for kk in range(TOP_K):
            _row_copy(h_ref, r, xs_ref, dest_ref[0, 0, kk * tm + r], sem).start(priority=kk % 2)
        return carry

    lax.fori_loop(0, tm, issue, 0, unroll=DMA_ISSUE_UNROLL)

    def fill(r, carry):
        _row_copy(zero_ref, 0, xs_ref, hole_ref[0, 0, r], sem).start()
        return carry

    lax.fori_loop(0, n_holes, fill, 0, unroll=DMA_ISSUE_UNROLL)
    for kk in range(TOP_K):
        pltpu.make_async_copy(h_ref, xs_ref.at[pl.ds(0, tm)], sem).wait()
    pltpu.make_async_copy(h_ref.at[pl.ds(0, n_holes)], xs_ref.at[pl.ds(0, n_holes)], sem).wait()


def _dispatch(h, dest_tiles, hole_tiles, n_rows):
    t, d = h.shape
    tm = dest_tiles.shape[-1] // TOP_K
    n_holes = hole_tiles.shape[-1]
    assert n_holes <= tm
    return pl.pallas_call(
        _dispatch_kernel,
        grid=(t // tm,),
        in_specs=[
            pl.BlockSpec((1, 1, TOP_K * tm), lambda i: (i, 0, 0), memory_space=pltpu.SMEM),
            pl.BlockSpec((1, 1, n_holes), lambda i: (i, 0, 0), memory_space=pltpu.SMEM),
            pl.BlockSpec((tm, d), lambda i: (i, 0)),
        ],
        out_specs=pl.BlockSpec(memory_space=pl.ANY),
        out_shape=jax.ShapeDtypeStruct((n_rows, d), _F32),
        scratch_shapes=[pltpu.VMEM((8, d), _F32), pltpu.SemaphoreType.DMA(())],
        compiler_params=_cparams(("arbitrary",)),
        name="moe_dispatch",
    )(dest_tiles, hole_tiles, h)


def _combine_kernel(dest_ref, dest_next_ref, ys_ref, gates_ref, x_ref, mod_ref, fg_ref, o_ref, buf_ref, sems,
                    *, final_norm):
    i = pl.program_id(0)
    tm = x_ref.shape[0]
    slot = i % 2

    def gather(idx_ref, s):
        def issue(r, carry):
            for kk in range(TOP_K):
                _row_copy(ys_ref, idx_ref[0, 0, kk * tm + r], buf_ref.at[s, kk], r, sems.at[s]).start(priority=kk % 2)
            return carry

        lax.fori_loop(0, tm, issue, 0, unroll=DMA_ISSUE_UNROLL)

    @pl.when(i == 0)
    def _():
        gather(dest_ref, 0)

    @pl.when(i + 1 < pl.num_programs(0))
    def _():
        gather(dest_next_ref, 1 - slot)

    for kk in range(TOP_K):
        pltpu.make_async_copy(ys_ref.at[pl.ds(0, tm)], buf_ref.at[slot, kk], sems.at[slot]).wait()
    y = gates_ref[:, 0:1] * buf_ref[slot, 0]
    for kk in range(1, TOP_K):
        y = y + gates_ref[:, kk:kk + 1] * buf_ref[slot, kk]
    xn = x_ref[...] + mod_ref[0][2:3, :] * y
    if final_norm:
        xn = xn * lax.rsqrt(jnp.mean(xn * xn, axis=-1, keepdims=True) + EPS) * fg_ref[...]
    o_ref[...] = xn


def _combine(ys, dest_tiles, gates, x2, mod, final_g, seq, final_norm):
    t, d = x2.shape
    tm = dest_tiles.shape[-1] // TOP_K
    tps = seq // tm
    n_i = t // tm
    return pl.pallas_call(
        functools.partial(_combine_kernel, final_norm=final_norm),
        grid=(n_i,),
        in_specs=[
            pl.BlockSpec((1, 1, TOP_K * tm), lambda i: (i, 0, 0), memory_space=pltpu.SMEM),
            pl.BlockSpec((1, 1, TOP_K * tm), lambda i: (jnp.minimum(i + 1, n_i - 1), 0, 0), memory_space=pltpu.SMEM),
            pl.BlockSpec(memory_space=pl.ANY),
            pl.BlockSpec((tm, TOP_K), lambda i: (i, 0)),
            pl.BlockSpec((tm, d), lambda i: (i, 0)),
            pl.BlockSpec((1, 3, d), lambda i: (i // tps, 0, 0)),
            pl.BlockSpec((1, d), lambda i: (0, 0)),
        ],
        out_specs=pl.BlockSpec((tm, d), lambda i: (i, 0)),
        out_shape=jax.ShapeDtypeStruct((t, d), _F32),
        scratch_shapes=[pltpu.VMEM((2, TOP_K, tm, d), _F32), pltpu.SemaphoreType.DMA((2,))],
        compiler_params=_cparams(("arbitrary",)),
        name="moe_combine",
    )(dest_tiles, dest_tiles, ys, gates, x2, mod, final_g)


def _experts_kernel(te_ref, nv_ref, xs_ref, wg_ref, wu_ref, wd_ref, o_ref, h_ref):
    i = pl.program_id(0)
    k = pl.program_id(1)

    @pl.when(k == 0)
    def _():
        o_ref[...] = jnp.zeros_like(o_ref)

    @pl.when(i < nv_ref[0])
    def _():
        @pl.when(k == 0)
        def _():
            h_ref[...] = xs_ref[...].astype(_BF16)

        o_ref[...] += _swiglu_tile(h_ref[...], wg_ref[0], wu_ref[0], wd_ref[0])


def _experts(xs, tile_expert, n_valid, w_gate, w_up, w_down, tm):
    r, d = xs.shape
    ne, _, f = w_gate.shape
    tf = _pick(f, HIDDEN_TILES_EXPERT)
    nk = f // tf
    n_tiles = r // tm

    def kk(i, k, nv):
        return jnp.where(i < nv[0], k, nk - 1)

    return pl.pallas_call(
        _experts_kernel,
        grid_spec=pltpu.PrefetchScalarGridSpec(
            num_scalar_prefetch=2,
            grid=(n_tiles, nk),
            in_specs=[
                pl.BlockSpec((tm, d), lambda i, k, te, nv: (jnp.minimum(i, nv[0] - 1), 0)),
                pl.BlockSpec((1, d, tf), lambda i, k, te, nv: (te[i], 0, kk(i, k, nv))),
                pl.BlockSpec((1, d, tf), lambda i, k, te, nv: (te[i], 0, kk(i, k, nv))),
                pl.BlockSpec((1, tf, d), lambda i, k, te, nv: (te[i], kk(i, k, nv), 0)),
            ],
            out_specs=pl.BlockSpec((tm, d), lambda i, k, te, nv: (i, 0)),
            scratch_shapes=[pltpu.VMEM((tm, d), _BF16)],
        ),
        out_shape=jax.ShapeDtypeStruct((r, d), _F32),
        compiler_params=_cparams(("arbitrary", "arbitrary")),
        name="moe_experts",
    )(tile_expert, n_valid, xs, w_gate, w_up, w_down)


def _moe_layer(x2, mod, norm_g, w_router, w_gate, w_up, w_down, final_g, seq, final_norm):
    t, d = x2.shape
    ne = w_router.shape[1]
    h, route, cnt = _route(x2, mod, norm_g, w_router.T.astype(_BF16), seq)
    tm = _pick(seq, ROW_TILES)
    counts = cnt[:, 0].astype(jnp.int32)
    padded = (counts + tm - 1) // tm * tm
    pend = jnp.cumsum(padded)
    pstart = pend - padded
    idx = route[0:TOP_K].astype(jnp.int32)
    rank = route[TOP_K:2 * TOP_K].astype(jnp.int32)
    dest = rank
    for e in range(ne):
        dest = dest + jnp.where(idx == e, pstart[e], 0)
    gates = route[2 * TOP_K:3 * TOP_K].T
    n_tiles = (TOP_K * t) // tm + ne
    dest_tiles = dest.reshape(TOP_K, t // tm, tm).transpose(1, 0, 2).reshape(t // tm, 1, TOP_K * tm)
    tile_start = jnp.arange(n_tiles, dtype=jnp.int32) * tm
    tile_expert = jnp.minimum(jnp.sum(pend[None, :] <= tile_start[:, None], axis=1), ne - 1).astype(jnp.int32)
    n_valid = (pend[-1] // tm).astype(jnp.int32).reshape(1)
    n_steps = t // tm
    holes_per_step = -(-(ne * tm) // n_steps)
    hid = jnp.minimum(jnp.arange(n_steps * holes_per_step, dtype=jnp.int32), ne * tm - 1)
    pad = padded - counts
    pad_end = jnp.cumsum(pad)
    hole = pend[-1] + (hid - pad_end[-1])
    for e in range(ne - 1, -1, -1):
        hole = jnp.where(hid < pad_end[e], pstart[e] + counts[e] + (hid - (pad_end[e] - pad[e])), hole)
    hole_tiles = hole.reshape(n_steps, 1, holes_per_step)
    xs = _dispatch(h, dest_tiles, hole_tiles, n_tiles * tm)
    ys = _experts(xs, tile_expert, n_valid, w_gate, w_up, w_down, tm)
    return _combine(ys, dest_tiles, gates, x2, mod, final_g, seq, final_norm)


def _final_norm_kernel(x_ref, g_ref, o_ref):
    x = x_ref[...]
    o_ref[...] = x * lax.rsqrt(jnp.mean(x * x, axis=-1, keepdims=True) + EPS) * g_ref[...]


def _final_norm(x2, fg):
    t, d = x2.shape
    tm = _pick(t, ROW_TILES)
    return pl.pallas_call(
        _final_norm_kernel,
        grid=(t // tm,),
        in_specs=[pl.BlockSpec((tm, d), lambda i: (i, 0)), pl.BlockSpec((1, d), lambda i: (0, 0))],
        out_specs=pl.BlockSpec((tm, d), lambda i: (i, 0)),
        out_shape=jax.ShapeDtypeStruct((t, d), _F32),
        compiler_params=_cparams(("arbitrary",)),
        name="final_norm",
    )(x2, fg)


def kernel(x, c, w_ada, b_ada, norm_g, w_in, ln_g, ln_b, w_s, b_s, w_pool, pool_scale, w_oa, w_ob, w_out,
           ffn_w1, ffn_w3, ffn_w2, w_router, moe_w_gate, moe_w_up, moe_w_down, final_g):
    bsz, seq, d = x.shape
    depth = w_in.shape[0]
    t = bsz * seq
    x2 = x.reshape(t, d)
    mods = _adaln_all(c, w_ada.reshape(depth * 2, d, 3 * d), b_ada.reshape(depth * 2, 1, 3 * d))
    mods = mods.reshape(depth * 2, bsz, 3, d)
    fg = final_g.reshape(1, d)
    bf = lambda w: w.astype(_BF16)
    moe_bf16 = {}
    w_in_b, w_pool_b, w_oa_b, w_ob_b, w_out_b = bf(w_in), bf(w_pool), bf(w_oa), bf(w_ob), bf(w_out)
    for i in range(depth):
        last = i == depth - 1
        a, b, gates = _inproj(x2, mods[2 * i], norm_g[i, 0].reshape(1, d), w_in_b,
                              ln_g[i].reshape(1, -1), ln_b[i].reshape(1, -1), w_s[i], b_s[i].T,
                              w_pool_b, pool_scale[i].reshape(1, -1), seq, layer=i)
        x2 = _outproj(a, b, gates, x2, mods[2 * i], w_oa_b, w_ob_b, w_out_b, seq, layer=i)
        j = i // 2
        ng = norm_g[i, 1].reshape(1, d)
        if i % 2 == 0:
            nxt = (i + 1) // 2
            pending = (moe_w_gate[nxt], moe_w_up[nxt], moe_w_down[nxt]) if i + 1 < depth else ()
            x2, converted = _dense_ffn(x2, mods[2 * i + 1], ng, bf(ffn_w1[j]), bf(ffn_w3[j]), bf(ffn_w2[j]),
                                       seq, cast_srcs=pending)
            if converted is not None:
                moe_bf16[nxt] = converted
            if last:
                x2 = _final_norm(x2, fg)
        else:
            wg, wu, wd = moe_bf16.get(j) or (bf(moe_w_gate[j]), bf(moe_w_up[j]), bf(moe_w_down[j]))
            x2 = _moe_layer(x2, mods[2 * i + 1], ng, w_router[j], wg, wu, wd, fg, seq, final_norm=last)
    return x2.reshape(bsz, seq, d)
```

```python
import functools

import jax
import jax.numpy as jnp
from jax import lax
from jax.experimental import pallas as pl
from jax.experimental.pallas import tpu as pltpu

EPS = 1e-6
POOL_WINDOWS = (2, 4, 8, 16)
TOP_K = 2
POOL_HALO = 16
V7X_MXU_COLS = 256
V7X_VMEM_LIMIT_BYTES = 56 * 1024 * 1024
V7X_VMEM_LIMIT_INPROJ_BYTES = 60 * 1024 * 1024
V7X_LANES = 128
BF16_SUBLANE_TILE = 16
DMA_ISSUE_UNROLL = 8
ROW_TILES_INPROJ = (1024, 512, 256, 128)
ROW_TILES = (512, 256, 128)
HIDDEN_TILES_DENSE = (512, 256, 128)
HIDDEN_TILES_EXPERT = (1024, 512, 256, 128)
OUT_COL_TILES = (512, 256, 128)
ADA_COL_TILES = (768, 512, 256, 128)

_BF16 = jnp.bfloat16
_F32 = jnp.float32


def _cparams(sem):
    return pltpu.CompilerParams(dimension_semantics=sem, vmem_limit_bytes=V7X_VMEM_LIMIT_BYTES)


def _gelu_tanh(x):
    return 0.5 * x * (1.0 + jnp.tanh(0.7978845608028654 * (x + 0.044715 * (x * x * x))))


def _silu(x):
    return x * jax.nn.sigmoid(x)


def _modnorm(x, g, shift, scale):
    y = x * lax.rsqrt(jnp.mean(x * x, axis=-1, keepdims=True) + EPS)
    return (y * g) * (1.0 + scale) + shift


def _pick(n, pref):
    for t in pref:
        if n % t == 0:
            return t
    return n


def _ada_kernel(c_ref, w_ref, b_ref, o_ref):
    sc = _silu(c_ref[...]).astype(_BF16)
    o_ref[0] = jnp.dot(sc, w_ref[0].astype(_BF16), preferred_element_type=_F32) + b_ref[0]


def _adaln_all(c, w_ada, b_ada):
    ns, d, d3 = w_ada.shape
    b = c.shape[0]
    tn = _pick(d3, ADA_COL_TILES)
    return pl.pallas_call(
        _ada_kernel,
        grid=(ns, d3 // tn),
        in_specs=[
            pl.BlockSpec((b, d), lambda s, j: (0, 0)),
            pl.BlockSpec((1, d, tn), lambda s, j: (s, 0, j)),
            pl.BlockSpec((1, 1, tn), lambda s, j: (s, 0, j)),
        ],
        out_specs=pl.BlockSpec((1, b, tn), lambda s, j: (s, 0, j)),
        out_shape=jax.ShapeDtypeStruct((ns, b, d3), _F32),
        compiler_params=_cparams(("arbitrary", "arbitrary")),
        name="adaln_mod",
    )(c, w_ada, b_ada)


def _inproj_kernel(*refs, n_cast, tiles_per_seq, heads, chunk, windows):
    x_ref, mod_ref, g_ref, w_ref, lng_ref, lnb_ref, ws_ref, bst_ref, wp_ref, ps_ref = refs[:10]
    cast_in = refs[10:10 + n_cast]
    a_ref, b_ref, gates_ref = refs[10 + n_cast:13 + n_cast]
    cast_out = refs[13 + n_cast:13 + 2 * n_cast]
    h_ref, u_ref, z_ref, vn_ref, halo_ref = refs[13 + 2 * n_cast:]
    for src, dst in zip(cast_in, cast_out):
        dst[...] = src[...].astype(dst.dtype)

    i = pl.program_id(0)
    j = pl.program_id(1)
    tm = x_ref.shape[0]
    tn = w_ref.shape[1]
    sw = min(V7X_MXU_COLS, tn)

    def zslab(lo, width):
        return jnp.dot(h_ref[...], w_ref[:, lo:lo + width], preferred_element_type=_F32)

    def sgu_epilogue():
        rsum = jnp.zeros((tm, 1), _F32)
        for lo in range(0, tn, sw):
            rsum = rsum + jnp.sum(z_ref[:, lo:lo + sw], axis=-1, keepdims=True)
        mu = rsum * (1.0 / tn)
        vsum = jnp.zeros((tm, 1), _F32)
        for lo in range(0, tn, sw):
            vc = z_ref[:, lo:lo + sw] - mu
            vsum = vsum + jnp.sum(vc * vc, axis=-1, keepdims=True)
        rstd = lax.rsqrt(vsum * (1.0 / tn) + EPS)
        for lo in range(0, tn, sw):
            cs = slice(lo, lo + sw)
            vn_ref[:, cs] = ((z_ref[:, cs] - mu) * rstd * lng_ref[:, cs] + lnb_ref[:, cs]).astype(_BF16)
        hd = tn // heads
        row = lax.broadcasted_iota(jnp.int32, (chunk, chunk), 0)
        col = lax.broadcasted_iota(jnp.int32, (chunk, chunk), 1)
        causal = col <= row
        for hh in range(heads):
            wm = jnp.where(causal, ws_ref[hh], 0.0).astype(_BF16)
            bias = bst_ref[:, hh:hh + 1]
            for cc in range(tm // chunk):
                rs = slice(cc * chunk, (cc + 1) * chunk)
                cs = slice(hh * hd, (hh + 1) * hd)
                mixed = jnp.dot(wm, vn_ref[rs, cs], preferred_element_type=_F32) + bias
                a_ref[rs, cs] = (u_ref[rs, cs].astype(_F32) * mixed).astype(a_ref.dtype)

    def pool_epilogue():
        first = (i % tiles_per_seq) == 0
        pos = (i % tiles_per_seq) * tm + lax.broadcasted_iota(jnp.int32, (tm, 1), 0) + 1
        gd = tn // len(windows)
        for gi, win in enumerate(windows):
            cs = slice(gi * gd, (gi + 1) * gd)
            z = z_ref[:, cs]
            hist = jnp.where(first, 0.0, halo_ref[:, cs])
            acc = jnp.concatenate([hist, z], axis=0)
            halo_ref[:, cs] = z[tm - POOL_HALO:, :]
            step = 1
            while step < win:
                acc = acc + pltpu.roll(acc, step, 0)
                step *= 2
            cnt = jnp.minimum(pos, win).astype(_F32)
            pooled = acc[POOL_HALO:, :] / cnt - z
            y = jnp.dot(pooled.astype(_BF16), wp_ref[gi], preferred_element_type=_F32)
            b_ref[:, cs] = (y * ps_ref[:, cs]).astype(b_ref.dtype)

    @pl.when(j == 0)
    def _():
        m = mod_ref[0]
        h_ref[...] = _modnorm(x_ref[...], g_ref[...], m[0:1, :], m[1:2, :]).astype(_BF16)
        for lo in range(0, tn, sw):
            u_ref[:, lo:lo + sw] = _gelu_tanh(zslab(lo, sw)).astype(u_ref.dtype)

    @pl.when(j == 1)
    def _():
        for lo in range(0, tn, sw):
            z_ref[:, lo:lo + sw] = _gelu_tanh(zslab(lo, sw))
        sgu_epilogue()

    @pl.when(j == 2)
    def _():
        for lo in range(0, tn, sw):
            z_ref[:, lo:lo + sw] = zslab(lo, sw)
        pool_epilogue()

    @pl.when(j >= 3)
    def _():
        for lo in range(0, tn, sw):
            gates_ref[:, lo:lo + sw] = jax.nn.sigmoid(zslab(lo, sw)).astype(gates_ref.dtype)


def _inproj_cast_ok(cast_src, n_i, n_j):
    rows, cols = cast_src.shape[0] * cast_src.shape[1], cast_src.shape[2]
    return (rows % n_i == 0 and cols % n_j == 0 and (rows // n_i) % BF16_SUBLANE_TILE == 0
            and (cols // n_j) % V7X_LANES == 0)


def _inproj(x2, mod, norm_g, w_in, ln_g, ln_b, w_s, b_s_t, w_pool, pool_scale, seq, layer, cast_src=None):
    t, d = x2.shape
    d_in = w_in.shape[2]
    tn = ln_g.shape[-1]
    heads, chunk, _ = w_s.shape
    _, groups, gd, _ = w_pool.shape
    assert pool_scale.shape[-1] == tn and groups * gd == tn and d % tn == 0
    assert d_in == 3 * tn + 2 * d and max(POOL_WINDOWS[:groups]) <= POOL_HALO
    tm = _pick(seq, ROW_TILES_INPROJ)
    assert tm % chunk == 0 and tm >= POOL_HALO
    n_i, n_j = t // tm, d_in // tn
    srcs2d, cast_specs = [], []
    if cast_src is not None and _inproj_cast_ok(cast_src, n_i, n_j):
        s2 = cast_src.reshape(-1, cast_src.shape[-1])
        srcs2d = [s2]
        cast_specs = [pl.BlockSpec((s2.shape[0] // n_i, s2.shape[1] // n_j), lambda i, j: (i, j))]
    kern = functools.partial(_inproj_kernel, n_cast=len(srcs2d), tiles_per_seq=seq // tm, heads=heads,
                             chunk=chunk, windows=POOL_WINDOWS[:groups])
    tps = seq // tm
    const2 = lambda i, j: (0, 0)
    const3 = lambda i, j: (0, 0, 0)
    outs = pl.pallas_call(
        kern,
        grid=(n_i, n_j),
        in_specs=[
            pl.BlockSpec((tm, d), lambda i, j: (i, 0)),
            pl.BlockSpec((1, 3, d), lambda i, j: (i // tps, 0, 0)),
            pl.BlockSpec((1, d), const2),
            pl.BlockSpec((None, d, tn), lambda i, j: (layer, 0, j)),
            pl.BlockSpec((1, tn), const2),
            pl.BlockSpec((1, tn), const2),
            pl.BlockSpec((heads, chunk, chunk), const3),
            pl.BlockSpec((chunk, heads), const2),
            pl.BlockSpec((None, groups, gd, gd), lambda i, j: (layer, 0, 0, 0)),
            pl.BlockSpec((1, tn), const2),
        ] + cast_specs,
        out_specs=[
            pl.BlockSpec((tm, tn), lambda i, j: (i, 0)),
            pl.BlockSpec((tm, tn), lambda i, j: (i, 0)),
            pl.BlockSpec((tm, tn), lambda i, j: (i, jnp.maximum(j - 3, 0))),
        ] + cast_specs,
        out_shape=[
            jax.ShapeDtypeStruct((t, tn), _BF16),
            jax.ShapeDtypeStruct((t, tn), _BF16),
            jax.ShapeDtypeStruct((t, 2 * d), _BF16),
        ] + [jax.ShapeDtypeStruct(s.shape, _BF16) for s in srcs2d],
        scratch_shapes=[
            pltpu.VMEM((tm, d), _BF16),
            pltpu.VMEM((tm, tn), _BF16),
            pltpu.VMEM((tm, tn), _F32),
            pltpu.VMEM((tm, tn), _BF16),
            pltpu.VMEM((POOL_HALO, tn), _F32),
        ],
        compiler_params=pltpu.CompilerParams(dimension_semantics=("arbitrary", "arbitrary"),
                                             vmem_limit_bytes=V7X_VMEM_LIMIT_INPROJ_BYTES),
        name="mixer_inproj",
    )(x2, mod, norm_g, w_in, ln_g, ln_b, w_s, b_s_t, w_pool, pool_scale, *srcs2d)
    converted = outs[3].reshape(cast_src.shape) if srcs2d else None
    return outs[0], outs[1], outs[2], converted


def _outproj_kernel(a_ref, b_ref, sga_ref, sgb_ref, x_ref, mod_ref, woa_ref, wob_ref, wout_ref,
                    o_ref, m_ref, *, tc):
    d = x_ref.shape[1]
    for c in range(d // tc):
        cs = slice(c * tc, (c + 1) * tc)
        ya = jnp.dot(a_ref[...], woa_ref[:, cs], preferred_element_type=_F32)
        yb = jnp.dot(b_ref[...], wob_ref[:, cs], preferred_element_type=_F32)
        m_ref[:, cs] = (sga_ref[:, cs].astype(_F32) * ya + sgb_ref[:, cs].astype(_F32) * yb).astype(_BF16)
    gate = mod_ref[0][2:3, :]
    for c in range(d // tc):
        cs = slice(c * tc, (c + 1) * tc)
        y = jnp.dot(m_ref[...], wout_ref[:, cs], preferred_element_type=_F32)
        o_ref[:, cs] = x_ref[:, cs] + gate[:, cs] * y


def _outproj(a, b, gates, x2, mod, w_oa, w_ob, w_out, seq, layer):
    t, d = x2.shape
    ds = a.shape[1]
    tm = _pick(seq, ROW_TILES)
    tc = _pick(d, OUT_COL_TILES)
    tps = seq // tm
    once = pl.Buffered(1)
    return pl.pallas_call(
        functools.partial(_outproj_kernel, tc=tc),
        grid=(t // tm,),
        in_specs=[
            pl.BlockSpec((tm, ds), lambda i: (i, 0)),
            pl.BlockSpec((tm, ds), lambda i: (i, 0)),
            pl.BlockSpec((tm, d), lambda i: (i, 0)),
            pl.BlockSpec((tm, d), lambda i: (i, 1)),
            pl.BlockSpec((tm, d), lambda i: (i, 0)),
            pl.BlockSpec((1, 3, d), lambda i: (i // tps, 0, 0)),
            pl.BlockSpec((None, ds, d), lambda i: (layer, 0, 0), pipeline_mode=once),
            pl.BlockSpec((None, ds, d), lambda i: (layer, 0, 0), pipeline_mode=once),
            pl.BlockSpec((None, d, d), lambda i: (layer, 0, 0), pipeline_mode=once),
        ],
        out_specs=pl.BlockSpec((tm, d), lambda i: (i, 0)),
        out_shape=jax.ShapeDtypeStruct((t, d), _F32),
        scratch_shapes=[pltpu.VMEM((tm, d), _BF16)],
        compiler_params=_cparams(("arbitrary",)),
        name="mixer_outproj",
    )(a, b, gates, gates, x2, mod, w_oa, w_ob, w_out)


def _swiglu_tile(h, wg, wu, wd):
    gte = jnp.dot(h, wg, preferred_element_type=_F32)
    up = jnp.dot(h, wu, preferred_element_type=_F32)
    act = (_silu(gte) * up).astype(_BF16)
    return jnp.dot(act, wd, preferred_element_type=_F32)


def _ffn_kernel(*refs, n_cast, cast_k, norm_k):
    x_ref, mod_ref, mcur_ref, g_ref, w1_ref, w3_ref, w2_ref = refs[:7]
    cast_in = refs[7:7 + n_cast]
    o_ref = refs[7 + n_cast]
    cast_out = refs[8 + n_cast:8 + 2 * n_cast]
    h_ref, hn_ref = refs[8 + 2 * n_cast:]
    i = pl.program_id(0)
    k = pl.program_id(1)

    def norm_next():
        m = mod_ref[0]
        hn_ref[...] = _modnorm(x_ref[...], g_ref[...], m[0:1, :], m[1:2, :]).astype(_BF16)

    def gated_tile(lhs_ref):
        return mcur_ref[0][2:3, :] * _swiglu_tile(lhs_ref[...], w1_ref[...], w3_ref[...], w2_ref[...])

    def cast_blocks():
        for src, dst in zip(cast_in, cast_out):
            dst[...] = src[...].astype(dst.dtype)

    @pl.when((i == 0) & (k == 0))
    def _():
        norm_next()

    @pl.when(k == 0)
    def _():
        o_ref[...] = x_ref[...] + gated_tile(hn_ref)
        h_ref[...] = hn_ref[...]
        if cast_k > 0:
            cast_blocks()

    @pl.when((k > 0) & (k < cast_k))
    def _():
        o_ref[...] += gated_tile(h_ref)
        cast_blocks()

    @pl.when(k == norm_k)
    def _():
        o_ref[...] += gated_tile(h_ref)
        norm_next()

    @pl.when((k >= max(cast_k, 1)) & (k != norm_k))
    def _():
        o_ref[...] += gated_tile(h_ref)


def _cast_plan(arrays, n_i, nk):
    for c in range(max(nk - 2, 0), 0, -1):
        if all(a.shape[0] % (n_i * c) == 0 and (a.shape[0] // (n_i * c)) % BF16_SUBLANE_TILE == 0 for a in arrays):
            return c
    return None


def _dense_ffn(x2, mod, norm_g, w1, w3, w2, seq, cast_srcs=()):
    t, d = x2.shape
    f = w1.shape[1]
    tm = _pick(seq, ROW_TILES)
    tf = _pick(f, HIDDEN_TILES_DENSE)
    tps = seq // tm
    n_i, nk = t // tm, f // tf
    srcs2d = [s.reshape(-1, s.shape[-1]) for s in cast_srcs]
    cast_k = _cast_plan(srcs2d, n_i, nk) if srcs2d else None
    if cast_k is None:
        srcs2d, cast_k = [], 0
    norm_k = nk - 1 if cast_k == 0 else max(cast_k, nk - 2)
    assert nk >= 2 and norm_k >= cast_k

    def x_tile(i, k):
        return jnp.minimum(i + (k >= norm_k).astype(jnp.int32), n_i - 1)

    def cast_block(i, k):
        return (i * cast_k + jnp.minimum(k, cast_k - 1), 0)

    cast_specs = [pl.BlockSpec((s.shape[0] // (n_i * cast_k), s.shape[1]), cast_block) for s in srcs2d]
    outs = pl.pallas_call(
        functools.partial(_ffn_kernel, n_cast=len(srcs2d), cast_k=cast_k, norm_k=norm_k),
        grid=(n_i, nk),
        in_specs=[
            pl.BlockSpec((tm, d), lambda i, k: (x_tile(i, k), 0)),
            pl.BlockSpec((1, 3, d), lambda i, k: (x_tile(i, k) // tps, 0, 0)),
            pl.BlockSpec((1, 3, d), lambda i, k: (i // tps, 0, 0)),
            pl.BlockSpec((1, d), lambda i, k: (0, 0)),
            pl.BlockSpec((d, tf), lambda i, k: (0, k)),
            pl.BlockSpec((d, tf), lambda i, k: (0, k)),
            pl.BlockSpec((tf, d), lambda i, k: (k, 0)),
        ] + cast_specs,
        out_specs=[pl.BlockSpec((tm, d), lambda i, k: (i, 0))] + cast_specs,
        out_shape=[jax.ShapeDtypeStruct((t, d), _F32)] + [jax.ShapeDtypeStruct(s.shape, _BF16) for s in srcs2d],
        scratch_shapes=[pltpu.VMEM((tm, d), _BF16), pltpu.VMEM((tm, d), _BF16)],
        compiler_params=_cparams(("arbitrary", "arbitrary")),
        name="dense_swiglu",
    )(x2, mod, mod, norm_g, w1, w3, w2, *srcs2d)
    if not srcs2d:
        return outs[0], None
    return outs[0], [o.reshape(s.shape) for o, s in zip(outs[1:], cast_srcs)]


def _route_kernel(x_ref, mod_ref, g_ref, wrt_ref, h_ref, route_ref, cnt_ref, run_ref):
    i = pl.program_id(0)
    tm = x_ref.shape[0]
    ne = wrt_ref.shape[0]

    @pl.when(i == 0)
    def _():
        run_ref[...] = jnp.zeros_like(run_ref)

    m = mod_ref[0]
    h = _modnorm(x_ref[...], g_ref[...], m[0:1, :], m[1:2, :])
    h_ref[...] = h
    lt = lax.dot_general(wrt_ref[...], h.astype(_BF16), (((1,), (1,)), ((), ())),
                         preferred_element_type=_F32)
    eid = lax.broadcasted_iota(jnp.int32, (ne, tm), 0).astype(_F32)
    m1 = jnp.max(lt, axis=0, keepdims=True)
    i1 = jnp.min(jnp.where(lt == m1, eid, float(ne)), axis=0, keepdims=True)
    l2 = jnp.where(eid == i1, -jnp.inf, lt)
    m2 = jnp.max(l2, axis=0, keepdims=True)
    i2 = jnp.min(jnp.where(l2 == m2, eid, float(ne)), axis=0, keepdims=True)
    e2 = jnp.exp(m2 - m1)
    g1 = 1.0 / (1.0 + e2)
    g2 = e2 / (1.0 + e2)
    oh1 = eid == i1
    oh2 = eid == i2
    both = jnp.where(oh1 | oh2, 1.0, 0.0).astype(_BF16)
    r = lax.broadcasted_iota(jnp.int32, (tm, tm), 0)
    c = lax.broadcasted_iota(jnp.int32, (tm, tm), 1)
    upper = jnp.where(r <= c, 1.0, 0.0).astype(_BF16)
    pref = jnp.dot(both, upper, preferred_element_type=_F32)
    base = run_ref[:, 0:1] + pref - 1.0
    rank1 = jnp.sum(jnp.where(oh1, base, 0.0), axis=0, keepdims=True)
    rank2 = jnp.sum(jnp.where(oh2, base, 0.0), axis=0, keepdims=True)
    run_new = run_ref[:, 0:1] + pref[:, tm - 1:tm]
    run_ref[...] = jnp.broadcast_to(run_new, run_ref.shape)
    cnt_ref[...] = jnp.broadcast_to(run_new, cnt_ref.shape)
    route_ref[0:1, :] = i1
    route_ref[1:2, :] = i2
    route_ref[2:3, :] = rank1
    route_ref[3:4, :] = rank2
    route_ref[4:5, :] = g1
    route_ref[5:6, :] = g2
    route_ref[6:8, :] = jnp.zeros((2, tm), _F32)


def _route(x2, mod, norm_g, w_router_t, seq):
    t, d = x2.shape
    ne = w_router_t.shape[0]
    tm = _pick(seq, ROW_TILES)
    tps = seq // tm
    return pl.pallas_call(
        _route_kernel,
        grid=(t // tm,),
        in_specs=[
            pl.BlockSpec((tm, d), lambda i: (i, 0)),
            pl.BlockSpec((1, 3, d), lambda i: (i // tps, 0, 0)),
            pl.BlockSpec((1, d), lambda i: (0, 0)),
            pl.BlockSpec((ne, d), lambda i: (0, 0)),
        ],
        out_specs=[
            pl.BlockSpec((tm, d), lambda i: (i, 0)),
            pl.BlockSpec((8, tm), lambda i: (0, i)),
            pl.BlockSpec((ne, V7X_LANES), lambda i: (0, 0)),
        ],
        out_shape=[
            jax.ShapeDtypeStruct((t, d), _F32),
            jax.ShapeDtypeStruct((8, t), _F32),
            jax.ShapeDtypeStruct((ne, V7X_LANES), _F32),
        ],
        scratch_shapes=[pltpu.VMEM((ne, V7X_LANES), _F32)],
        compiler_params=_cparams(("arbitrary",)),
        name="moe_route",
    )(x2, mod, norm_g, w_router_t)


def _row_copy(src_ref, src_row, dst_ref, dst_row, sem):
    return pltpu.make_async_copy(src_ref.at[pl.ds(src_row, 1)], dst_ref.at[pl.ds(dst_row, 1)], sem)


def _dispatch_kernel(dest_ref, hole_ref, h_ref, xs_ref, zero_ref, sem):
    tm = h_ref.shape[0]
    n_holes = hole_ref.shape[-1]
    zero_ref[...] = jnp.zeros_like(zero_ref)

    def issue(r, carry):
        for kk in range(TOP_K):
            _row_copy(h_ref, r, xs_ref, dest_ref[0, 0, kk * tm + r], sem).start(priority=kk % 2)
        return carry

    lax.fori_loop(0, tm, issue, 0, unroll=DMA_ISSUE_UNROLL)

    def fill(r, carry):
        _row_copy(zero_ref, 0, xs_ref, hole_ref[0, 0, r], sem).start()
        return carry

    lax.fori_loop(0, n_holes, fill, 0, unroll=DMA_ISSUE_UNROLL)
    for kk in range(TOP_K):
        pltpu.make_async_copy(h_ref, xs_ref.at[pl.ds(0, tm)], sem).wait()
    pltpu.make_async_copy(h_ref.at[pl.ds(0, n_holes)], xs_ref.at[pl.ds(0, n_holes)], sem).wait()


def _dispatch(h, dest_tiles, hole_tiles, n_rows):
    t, d = h.shape
    tm = dest_tiles.shape[-1] // TOP_K
    n_holes = hole_tiles.shape[-1]
    assert n_holes <= tm
    return pl.pallas_call(
        _dispatch_kernel,
        grid=(t // tm,),
        in_specs=[
            pl.BlockSpec((1, 1, TOP_K * tm), lambda i: (i, 0, 0), memory_space=pltpu.SMEM),
            pl.BlockSpec((1, 1, n_holes), lambda i: (i, 0, 0), memory_space=pltpu.SMEM),
            pl.BlockSpec((tm, d), lambda i: (i, 0)),
        ],
        out_specs=pl.BlockSpec(memory_space=pl.ANY),
        out_shape=jax.ShapeDtypeStruct((n_rows, d), _F32),
        scratch_shapes=[pltpu.VMEM((8, d), _F32), pltpu.SemaphoreType.DMA(())],
        compiler_params=_cparams(("arbitrary",)),
        name="moe_dispatch",
    )(dest_tiles, hole_tiles, h)


def _combine_kernel(dest_ref, dest_next_ref, ys_ref, gates_ref, x_ref, mod_ref, fg_ref, o_ref, buf_ref, sems,
                    *, final_norm):
    i = pl.program_id(0)
    tm = x_ref.shape[0]
    slot = i % 2

    def gather(idx_ref, s):
        def issue(r, carry):
            for kk in range(TOP_K):
                _row_copy(ys_ref, idx_ref[0, 0, kk * tm + r], buf_ref.at[s, kk], r, sems.at[s]).start(priority=kk % 2)
            return carry

        lax.fori_loop(0, tm, issue, 0, unroll=DMA_ISSUE_UNROLL)

    @pl.when(i == 0)
    def _():
        gather(dest_ref, 0)

    @pl.when(i + 1 < pl.num_programs(0))
    def _():
        gather(dest_next_ref, 1 - slot)

    for kk in range(TOP_K):
        pltpu.make_async_copy(ys_ref.at[pl.ds(0, tm)], buf_ref.at[slot, kk], sems.at[slot]).wait()
    y = gates_ref[:, 0:1] * buf_ref[slot, 0]
    for kk in range(1, TOP_K):
        y = y + gates_ref[:, kk:kk + 1] * buf_ref[slot, kk]
    xn = x_ref[...] + mod_ref[0][2:3, :] * y
    if final_norm:
        xn = xn * lax.rsqrt(jnp.mean(xn * xn, axis=-1, keepdims=True) + EPS) * fg_ref[...]
    o_ref[...] = xn


def _combine(ys, dest_tiles, gates, x2, mod, final_g, seq, final_norm):
    t, d = x2.shape
    tm = dest_tiles.shape[-1] // TOP_K
    tps = seq // tm
    n_i = t // tm
    return pl.pallas_call(
        functools.partial(_combine_kernel, final_norm=final_norm),
        grid=(n_i,),
        in_specs=[
            pl.BlockSpec((1, 1, TOP_K * tm), lambda i: (i, 0, 0), memory_space=pltpu.SMEM),
            pl.BlockSpec((1, 1, TOP_K * tm), lambda i: (jnp.minimum(i + 1, n_i - 1), 0, 0), memory_space=pltpu.SMEM),
            pl.BlockSpec(memory_space=pl.ANY),
            pl.BlockSpec((tm, TOP_K), lambda i: (i, 0)),
            pl.BlockSpec((tm, d), lambda i: (i, 0)),
            pl.BlockSpec((1, 3, d), lambda i: (i // tps, 0, 0)),
            pl.BlockSpec((1, d), lambda i: (0, 0)),
        ],
        out_specs=pl.BlockSpec((tm, d), lambda i: (i, 0)),
        out_shape=jax.ShapeDtypeStruct((t, d), _F32),
        scratch_shapes=[pltpu.VMEM((2, TOP_K, tm, d), _F32), pltpu.SemaphoreType.DMA((2,))],
        compiler_params=_cparams(("arbitrary",)),
        name="moe_combine",
    )(dest_tiles, dest_tiles, ys, gates, x2, mod, final_g)


def _experts_kernel(te_ref, nv_ref, xs_ref, wg_ref, wu_ref, wd_ref, o_ref, h_ref):
    i = pl.program_id(0)
    k = pl.program_id(1)

    @pl.when(k == 0)
    def _():
        o_ref[...] = jnp.zeros_like(o_ref)

    @pl.when(i < nv_ref[0])
    def _():
        @pl.when(k == 0)
        def _():
            h_ref[...] = xs_ref[...].astype(_BF16)

        o_ref[...] += _swiglu_tile(h_ref[...], wg_ref[0], wu_ref[0], wd_ref[0])


def _experts(xs, tile_expert, n_valid, w_gate, w_up, w_down, tm):
    r, d = xs.shape
    ne, _, f = w_gate.shape
    tf = _pick(f, HIDDEN_TILES_EXPERT)
    nk = f // tf
    n_tiles = r // tm

    def kk(i, k, nv):
        return jnp.where(i < nv[0], k, nk - 1)

    return pl.pallas_call(
        _experts_kernel,
        grid_spec=pltpu.PrefetchScalarGridSpec(
            num_scalar_prefetch=2,
            grid=(n_tiles, nk),
            in_specs=[
                pl.BlockSpec((tm, d), lambda i, k, te, nv: (jnp.minimum(i, nv[0] - 1), 0)),
                pl.BlockSpec((1, d, tf), lambda i, k, te, nv: (te[i], 0, kk(i, k, nv))),
                pl.BlockSpec((1, d, tf), lambda i, k, te, nv: (te[i], 0, kk(i, k, nv))),
                pl.BlockSpec((1, tf, d), lambda i, k, te, nv: (te[i], kk(i, k, nv), 0)),
            ],
            out_specs=pl.BlockSpec((tm, d), lambda i, k, te, nv: (i, 0)),
            scratch_shapes=[pltpu.VMEM((tm, d), _BF16)],
        ),
        out_shape=jax.ShapeDtypeStruct((r, d), _F32),
        compiler_params=_cparams(("arbitrary", "arbitrary")),
        name="moe_experts",
    )(tile_expert, n_valid, xs, w_gate, w_up, w_down)


def _moe_layer(x2, mod, norm_g, w_router, w_gate, w_up, w_down, final_g, seq, final_norm):
    t, d = x2.shape
    ne = w_router.shape[1]
    h, route, cnt = _route(x2, mod, norm_g, w_router.T.astype(_BF16), seq)
    tm = _pick(seq, ROW_TILES)
    counts = cnt[:, 0].astype(jnp.int32)
    padded = (counts + tm - 1) // tm * tm
    pend = jnp.cumsum(padded)
    pstart = pend - padded
    idx = route[0:TOP_K].astype(jnp.int32)
    rank = route[TOP_K:2 * TOP_K].astype(jnp.int32)
    dest = rank
    for e in range(ne):
        dest = dest + jnp.where(idx == e, pstart[e], 0)
    gates = route[2 * TOP_K:3 * TOP_K].T
    n_tiles = (TOP_K * t) // tm + ne
    dest_tiles = dest.reshape(TOP_K, t // tm, tm).transpose(1, 0, 2).reshape(t // tm, 1, TOP_K * tm)
    tile_start = jnp.arange(n_tiles, dtype=jnp.int32) * tm
    tile_expert = jnp.minimum(jnp.sum(pend[None, :] <= tile_start[:, None], axis=1), ne - 1).astype(jnp.int32)
    n_valid = (pend[-1] // tm).astype(jnp.int32).reshape(1)
    n_steps = t // tm
    holes_per_step = -(-(ne * tm) // n_steps)
    hid = jnp.minimum(jnp.arange(n_steps * holes_per_step, dtype=jnp.int32), ne * tm - 1)
    pad = padded - counts
    pad_end = jnp.cumsum(pad)
    hole = pend[-1] + (hid - pad_end[-1])
    for e in range(ne - 1, -1, -1):
        hole = jnp.where(hid < pad_end[e], pstart[e] + counts[e] + (hid - (pad_end[e] - pad[e])), hole)
    hole_tiles = hole.reshape(n_steps, 1, holes_per_step)
    xs = _dispatch(h, dest_tiles, hole_tiles, n_tiles * tm)
    ys = _experts(xs, tile_expert, n_valid, w_gate, w_up, w_down, tm)
    return _combine(ys, dest_tiles, gates, x2, mod, final_g, seq, final_norm)


def _final_norm_kernel(x_ref, g_ref, o_ref):
    x = x_ref[...]
    o_ref[...] = x * lax.rsqrt(jnp.mean(x * x, axis=-1, keepdims=True) + EPS) * g_ref[...]


def _final_norm(x2, fg):
    t, d = x2.shape
    tm = _pick(t, ROW_TILES)
    return pl.pallas_call(
        _final_norm_kernel,
        grid=(t // tm,),
        in_specs=[pl.BlockSpec((tm, d), lambda i: (i, 0)), pl.BlockSpec((1, d), lambda i: (0, 0))],
        out_specs=pl.BlockSpec((tm, d), lambda i: (i, 0)),
        out_shape=jax.ShapeDtypeStruct((t, d), _F32),
        compiler_params=_cparams(("arbitrary",)),
        name="final_norm",
    )(x2, fg)


def kernel(x, c, w_ada, b_ada, norm_g, w_in, ln_g, ln_b, w_s, b_s, w_pool, pool_scale, w_oa, w_ob, w_out,
           ffn_w1, ffn_w3, ffn_w2, w_router, moe_w_gate, moe_w_up, moe_w_down, final_g):
    bsz, seq, d = x.shape
    depth = w_in.shape[0]
    t = bsz * seq
    x2 = x.reshape(t, d)
    mods = _adaln_all(c, w_ada.reshape(depth * 2, d, 3 * d), b_ada.reshape(depth * 2, 1, 3 * d))
    mods = mods.reshape(depth * 2, bsz, 3, d)
    fg = final_g.reshape(1, d)
    bf = lambda w: w.astype(_BF16)
    moe_bf16 = {}
    w_in_b, w_pool_b, w_oa_b, w_ob_b, w_out_b = bf(w_in), bf(w_pool), bf(w_oa), bf(w_ob), bf(w_out)
    for i in range(depth):
        last = i == depth - 1
        j = i // 2
        if i % 2 == 0:
            in_cast = ("gate", (i + 1) // 2, moe_w_gate) if i + 1 < depth else None
        else:
            in_cast = ("up", j, moe_w_up) if i >= 1 else None
        a, b, gates, conv = _inproj(x2, mods[2 * i], norm_g[i, 0].reshape(1, d), w_in_b,
                                    ln_g[i].reshape(1, -1), ln_b[i].reshape(1, -1), w_s[i], b_s[i].T,
                                    w_pool_b, pool_scale[i].reshape(1, -1), seq, layer=i,
                                    cast_src=in_cast[2][in_cast[1]] if in_cast else None)
        if conv is not None:
            moe_bf16[(in_cast[0], in_cast[1])] = conv
        x2 = _outproj(a, b, gates, x2, mods[2 * i], w_oa_b, w_ob_b, w_out_b, seq, layer=i)
        ng = norm_g[i, 1].reshape(1, d)
        if i % 2 == 0:
            nxt = (i + 1) // 2
            pending = (moe_w_down[nxt],) if i + 1 < depth else ()
            x2, converted = _dense_ffn(x2, mods[2 * i + 1], ng, bf(ffn_w1[j]), bf(ffn_w3[j]), bf(ffn_w2[j]),
                                       seq, cast_srcs=pending)
            if converted is not None:
                moe_bf16[("down", nxt)] = converted[0]
            if last:
                x2 = _final_norm(x2, fg)
        else:
            wg = moe_bf16[("gate", j)] if ("gate", j) in moe_bf16 else bf(moe_w_gate[j])
            wu = moe_bf16[("up", j)] if ("up", j) in moe_bf16 else bf(moe_w_up[j])
            wd = moe_bf16[("down", j)] if ("down", j) in moe_bf16 else bf(moe_w_down[j])
            x2 = _moe_layer(x2, mods[2 * i + 1], ng, w_router[j], wg, wu, wd, fg, seq, final_norm=last)
    return x2.reshape(bsz, seq, d)
```
